```python
import math
import jax
import jax.numpy as jnp
from jax import lax
import numpy as np

D_MODEL = 1024
BATCH = 16
SEQ = 2048
DEPTH = 2
DEC_BATCH = 128
DEC_SEQ = 1
PAST_LEN = 16384
PAGE_SIZE = 128

HEAD_DIM = 64
ROT_DIM = HEAD_DIM // 4
ROPE_THETA = 500000.0
BRANCH_W = 384
N_HEADS = BRANCH_W // HEAD_DIM
N_KV = 2
QW = N_HEADS * HEAD_DIM
KVW = N_KV * HEAD_DIM
WIN_A = 128
DIL_GROUPS = ((128, 1), (512, 4), (2048, 16))
N_DIL = len(DIL_GROUPS)
CONV_W = 3
CHUNK = 128
N_SG = 6
N_BRANCH = 4
N_IN = (QW + 2 * KVW) * (1 + N_DIL) + 5 * BRANCH_W
D_FF = -(-(8 * D_MODEL) // (3 * 256)) * 256
Q_BLOCK = 128
EPS = 1e-6

kernel_name = 'hybrid_gated_parallel_mixer_decode_step'


def rms_norm(x, g):
    xf = x.astype(jnp.float32)
    xf = xf * lax.rsqrt(jnp.mean(xf * xf, axis=-1, keepdims=True) + EPS)
    return (xf * g.astype(jnp.float32)).astype(x.dtype)


def partial_rope(x, pos):
    half = ROT_DIM // 2
    inv_freq = ROPE_THETA ** (-2.0 * jnp.arange(half, dtype=jnp.float32) / ROT_DIM)
    ang = pos[:, None] * inv_freq[None, :]
    cos = jnp.cos(ang)[None, :, None, :]
    sin = jnp.sin(ang)[None, :, None, :]
    xf = x.astype(jnp.float32)
    x1 = xf[..., :half]
    x2 = xf[..., half:ROT_DIM]
    out = jnp.concatenate([x1 * cos - x2 * sin, x2 * cos + x1 * sin, xf[..., ROT_DIM:]], axis=-1)
    return out.astype(x.dtype)


def split_columns(z):
    sizes = [QW, KVW, KVW] * (1 + N_DIL) + [BRANCH_W] * 5
    out, o = [], 0
    for s in sizes:
        out.append(z[..., o:o + s])
        o += s
    return out


def attn_heads(q, k, v, qn, kn, pos):
    B, S, _ = q.shape
    q = partial_rope(rms_norm(q.reshape(B, S, N_HEADS, HEAD_DIM), qn), pos)
    k = partial_rope(rms_norm(k.reshape(B, S, N_KV, HEAD_DIM), kn), pos)
    return q, k, v.reshape(B, S, N_KV, HEAD_DIM)


def sliding_window_attn(q, k, v, n_buf, sink):
    B, S, H, dh = q.shape
    G = H // N_KV
    pad = WIN_A - n_buf
    kp = jnp.pad(k, ((0, 0), (pad, 0), (0, 0), (0, 0)))
    vp = jnp.pad(v, ((0, 0), (pad, 0), (0, 0), (0, 0)))
    qb = math.gcd(S, Q_BLOCK)
    nb = S // qb
    span = WIN_A + qb
    qg = q.reshape(B, nb, qb, N_KV, G, dh)
    sink_l = jnp.broadcast_to(sink.astype(jnp.float32).reshape(1, N_KV, G, 1, 1), (B, N_KV, G, qb, 1))
    rel = WIN_A + jnp.arange(qb)[:, None] - jnp.arange(span)[None, :]
    in_band = (rel >= 0) & (rel <= WIN_A)
    scale = 1.0 / math.sqrt(dh)

    def block(i):
        start = i * qb
        kb = lax.dynamic_slice_in_dim(kp, start, span, axis=1)
        vb = lax.dynamic_slice_in_dim(vp, start, span, axis=1)
        s = jnp.einsum('bqkgd,bskd->bkgqs', qg[:, i], kb, preferred_element_type=jnp.float32) * scale
        real = (start + jnp.arange(span)) >= pad
        s = jnp.where(in_band & real[None, :], s, -jnp.inf)
        p = jax.nn.softmax(jnp.concatenate([s, sink_l], axis=-1), axis=-1)[..., :span]
        return jnp.einsum('bkgqs,bskd->bqkgd', p.astype(v.dtype), vb)

    o = lax.map(block, jnp.arange(nb))
    return jnp.moveaxis(o, 0, 1).reshape(B, S, H, dh)


def dilated_attn(q, k, v, n_buf, window, dil):
    B, S, H, dh = q.shape
    G = H // N_KV
    pad = window - n_buf
    kp = jnp.pad(k, ((0, 0), (pad, 0), (0, 0), (0, 0)))
    vp = jnp.pad(v, ((0, 0), (pad, 0), (0, 0), (0, 0)))
    nk = window // dil + 1
    qb = math.gcd(S, Q_BLOCK)
    nb = S // qb
    qg = q.reshape(B, nb, qb, N_KV, G, dh)
    steps = dil * jnp.arange(nk)
    scale = 1.0 / math.sqrt(dh)

    def block(i):
        kidx = i * qb + window + jnp.arange(qb)[:, None] - steps[None, :]
        kb = jnp.take(kp, kidx, axis=1)
        vb = jnp.take(vp, kidx, axis=1)
        s = jnp.einsum('bqkgd,bqnkd->bkgqn', qg[:, i], kb, preferred_element_type=jnp.float32) * scale
        s = jnp.where(kidx >= pad, s, -jnp.inf)
        lse = jax.nn.logsumexp(s, axis=-1)
        p = jnp.exp(s - lse[..., None])
        o = jnp.einsum('bkgqn,bqnkd->bqkgd', p.astype(v.dtype), vb)
        return o, jnp.moveaxis(lse, 3, 1)

    o, lse = lax.map(block, jnp.arange(nb))
    o = jnp.moveaxis(o, 0, 1).reshape(B, S, H, dh)
    lse = jnp.moveaxis(lse, 0, 1).reshape(B, S, H)
    return o, lse


def chunk_spatial_gate(u, v, ws, bs):
    B, S, C = v.shape
    sp = -(-S // CHUNK) * CHUNK
    vp = jnp.pad(v, ((0, 0), (0, sp - S), (0, 0))).reshape(B, sp // CHUNK, CHUNK, N_SG, C // N_SG)
    tri = jnp.tril(jnp.ones((CHUNK, CHUNK), dtype=bool))
    wm = jnp.where(tri[None], ws, 0.0).astype(v.dtype)
    sv = jnp.einsum('gij,bcjge->bcige', wm, vp) + bs.T.astype(v.dtype)[None, None, :, :, None]
    return u * sv.reshape(B, sp, C)[:, :S]


def trunk_layer(x, pos, buf_a, bufs_b, buf_c, ln1, w_in, q_norm_a, k_norm_a, sink_a, q_norm_b, k_norm_b,
                conv_c, ws_d, bs_d, w_gate, b_gate, w_branch, w_o, ln2, w_ffn_in, w_ffn_out):
    B, S, _ = x.shape
    h = rms_norm(x, ln1)
    cols = split_columns(h @ w_in)
    q, k, v = attn_heads(cols[0], cols[1], cols[2], q_norm_a, k_norm_a, pos)
    kv = jnp.concatenate([buf_a, jnp.stack([k, v], axis=1)], axis=2)
    o_a = sliding_window_attn(q, kv[:, 0], kv[:, 1], buf_a.shape[2], sink_a)
    new_a = kv[:, :, kv.shape[2] - min(WIN_A, kv.shape[2]):]
    outs, lses, new_b = [], [], []
    for g, (window, dil) in enumerate(DIL_GROUPS):
        c = 3 + 3 * g
        q, k, v = attn_heads(cols[c], cols[c + 1], cols[c + 2], q_norm_b[g], k_norm_b[g], pos)
        kv = jnp.concatenate([bufs_b[g], jnp.stack([k, v], axis=1)], axis=2)
        o, lse = dilated_attn(q, kv[:, 0], kv[:, 1], bufs_b[g].shape[2], window, dil)
        outs.append(o)
        lses.append(lse)
        new_b.append(kv[:, :, kv.shape[2] - min(window, kv.shape[2]):])
    wts = jax.nn.softmax(jnp.stack(lses), axis=0)
    o_b = jnp.einsum('gbsh,gbshd->bshd', wts, jnp.stack(outs).astype(jnp.float32)).astype(x.dtype)
    c = 3 + 3 * N_DIL
    gate_b, gate_c, x_c = cols[c], cols[c + 1], cols[c + 2]
    zc = jnp.concatenate([buf_c, gate_c * x_c], axis=1)
    conv = sum(conv_c[j] * zc[:, j:j + S] for j in range(CONV_W))
    o_c = gate_b * conv
    new_c = zc[:, S:]
    u_d, v_d = cols[c + 3], cols[c + 4]
    o_d = chunk_spatial_gate(u_d, v_d, ws_d, bs_d)
    branches = (o_a.reshape(B, S, BRANCH_W), o_b.reshape(B, S, BRANCH_W), o_c, o_d)
    merged = sum(jax.nn.sigmoid((h @ w_gate[n] + b_gate[n]).astype(jnp.float32)).astype(x.dtype) * (br @ w_branch[n])
                 for n, br in enumerate(branches))
    x = x + merged @ w_o
    h2 = rms_norm(x, ln2)
    gt, up = jnp.split(h2 @ w_ffn_in, 2, axis=-1)
    x = x + (jax.nn.silu(gt) * up) @ w_ffn_out
    return x, new_a, new_b, new_c, v_d


def setup_inputs(seed: int = 0) -> dict:
    key = jax.random.key(seed)
    ks = jax.random.split(key, 24)
    f32 = jnp.float32

    def normal(k, shape, scale):
        return scale * jax.random.normal(k, shape, f32)

    def gain(k, shape):
        return 1.0 + 0.1 * jax.random.normal(k, shape, f32)

    la = min(WIN_A, PAST_LEN)
    lb = [min(w, PAST_LEN) for w, _ in DIL_GROUPS]
    return {
        'x_prompt': normal(ks[0], (BATCH, SEQ, D_MODEL), 1.0),
        'x_sample': normal(ks[1], (DEC_BATCH, DEC_SEQ, D_MODEL), 1.0),
        'cache_a': normal(ks[2], (DEPTH, DEC_BATCH, 2, la, N_KV, HEAD_DIM), 1.0),
        'cache_b1': normal(ks[3], (DEPTH, DEC_BATCH, 2, lb[0], N_KV, HEAD_DIM), 1.0),
        'cache_b2': normal(ks[4], (DEPTH, DEC_BATCH, 2, lb[1], N_KV, HEAD_DIM), 1.0),
        'cache_b3': normal(ks[5], (DEPTH, DEC_BATCH, 2, lb[2], N_KV, HEAD_DIM), 1.0),
        'state_c': normal(ks[6], (DEPTH, DEC_BATCH, CONV_W - 1, BRANCH_W), 1.0),
        'ln1': gain(ks[7], (DEPTH, D_MODEL)),
        'w_in': normal(ks[8], (DEPTH, D_MODEL, N_IN), D_MODEL ** -0.5),
        'q_norm_a': gain(ks[9], (DEPTH, HEAD_DIM)),
        'k_norm_a': gain(ks[10], (DEPTH, HEAD_DIM)),
        'sink_a': normal(ks[11], (DEPTH, N_HEADS), 1.0),
        'q_norm_b': gain(ks[12], (DEPTH, N_DIL, HEAD_DIM)),
        'k_norm_b': gain(ks[13], (DEPTH, N_DIL, HEAD_DIM)),
        'conv_c': normal(ks[14], (DEPTH, CONV_W, BRANCH_W), CONV_W ** -0.5),
        'ws_d': normal(ks[15], (DEPTH, N_SG, CHUNK, CHUNK), CHUNK ** -0.5),
        'bs_d': 1.0 + normal(ks[16], (DEPTH, N_SG, CHUNK), 0.1),
        'w_gate': normal(ks[17], (DEPTH, N_BRANCH, D_MODEL, D_MODEL), D_MODEL ** -0.5),
        'b_gate': normal(ks[18], (DEPTH, N_BRANCH, D_MODEL), 0.02),
        'w_branch': normal(ks[19], (DEPTH, N_BRANCH, BRANCH_W, D_MODEL), BRANCH_W ** -0.5),
        'w_o': normal(ks[20], (DEPTH, D_MODEL, D_MODEL), D_MODEL ** -0.5),
        'ln2': gain(ks[21], (DEPTH, D_MODEL)),
        'w_ffn_in': normal(ks[22], (DEPTH, D_MODEL, 2 * D_FF), D_MODEL ** -0.5),
        'w_ffn_out': normal(ks[23], (DEPTH, D_FF, D_MODEL), D_FF ** -0.5),
    }


def reference(x_prompt, x_sample, cache_a, cache_b1, cache_b2, cache_b3, state_c, ln1, w_in, q_norm_a, k_norm_a,
              sink_a, q_norm_b, k_norm_b, conv_c, ws_d, bs_d, w_gate, b_gate, w_branch, w_o, ln2, w_ffn_in,
              w_ffn_out):
    bp, sp, _ = x_prompt.shape
    ds = x_sample.shape[1]
    dt = x_prompt.dtype
    pos_p = jnp.arange(sp, dtype=jnp.float32)
    pos_s = PAST_LEN + jnp.arange(ds, dtype=jnp.float32)
    empty_kv = jnp.zeros((bp, 2, 0, N_KV, HEAD_DIM), dt)
    zero_conv = jnp.zeros((bp, CONV_W - 1, BRANCH_W), dt)
    caches_b = (cache_b1, cache_b2, cache_b3)
    y_p, y_s = x_prompt, x_sample
    a_p, a_s, c_p, c_s, d_s = [], [], [], [], []
    b_p = [[] for _ in DIL_GROUPS]
    b_s = [[] for _ in DIL_GROUPS]
    for l in range(DEPTH):
        lw = (ln1[l], w_in[l], q_norm_a[l], k_norm_a[l], sink_a[l], q_norm_b[l], k_norm_b[l], conv_c[l],
              ws_d[l], bs_d[l], w_gate[l], b_gate[l], w_branch[l], w_o[l], ln2[l], w_ffn_in[l], w_ffn_out[l])
        y_p, na, nb, nc, _ = trunk_layer(y_p, pos_p, empty_kv, (empty_kv,) * N_DIL, zero_conv, *lw)
        y_s, na2, nb2, nc2, vd = trunk_layer(y_s, pos_s, cache_a[l], tuple(cb[l] for cb in caches_b),
                                             state_c[l], *lw)
        a_p.append(na)
        a_s.append(na2)
        for g in range(N_DIL):
            b_p[g].append(nb[g])
            b_s[g].append(nb2[g])
        c_p.append(nc)
        c_s.append(nc2)
        d_s.append(vd)
    return (y_p, y_s, jnp.stack(a_p), jnp.stack(a_s), jnp.stack(b_p[0]), jnp.stack(b_s[0]),
            jnp.stack(b_p[1]), jnp.stack(b_s[1]), jnp.stack(b_p[2]), jnp.stack(b_s[2]),
            jnp.stack(c_p), jnp.stack(c_s), jnp.stack(d_s))
```

```python
import functools
import math

import jax
import jax.numpy as jnp
from jax import lax
from jax.experimental import pallas as pl
from jax.experimental.pallas import tpu as pltpu

F32 = jnp.float32
BF16 = jnp.bfloat16

D_MODEL = 1024
HEAD_DIM = 64
ROT_DIM = HEAD_DIM // 4
ROT_HALF = ROT_DIM // 2
ROPE_THETA = 500000.0
BRANCH_W = 384
N_HEADS = BRANCH_W // HEAD_DIM
N_KV = 2
N_GROUP = N_HEADS // N_KV
QW = N_HEADS * HEAD_DIM
KVW = N_KV * HEAD_DIM
ATT_W = QW + 2 * KVW
WIN_A = 128
DIL_GROUPS = ((128, 1), (512, 4), (2048, 16))
MIXERS = ((WIN_A, 1),) + DIL_GROUPS
N_MIX = len(MIXERS)
BAND = 128
CONV_W = 3
CHUNK = 128
N_SG = 6
N_BRANCH = 4
N_ATT = ATT_W * N_MIX
N_IN = N_ATT + 5 * BRANCH_W
D_FF = 2816
FF_CHUNK = 256
PAST_LEN = 16384
EPS = 1e-6
NEG = -1e30

VMEM_LIMIT_V7X = 56 * 1024 * 1024
ROW_TILE = 512
DEC_BLOCK = 2


def _params(n_axes, vmem=VMEM_LIMIT_V7X):
    return pltpu.CompilerParams(dimension_semantics=("arbitrary",) * n_axes, vmem_limit_bytes=vmem)


def _const_spec(shape):
    nd = len(shape)
    return pl.BlockSpec(shape, lambda *_: (0,) * nd)


def _rms(x, g):
    return x * lax.rsqrt(jnp.mean(x * x, axis=-1, keepdims=True) + EPS) * g


def _dot(a, b):
    return jnp.dot(a, b, preferred_element_type=F32)


def _project(x_ref, ln_ref, w_ref, h_ref, zatt_ref):
    h = _rms(x_ref[...], ln_ref[...]).astype(BF16)
    h_ref[...] = h
    for m in range(N_MIX):
        zatt_ref[:, m * ATT_W:(m + 1) * ATT_W] = _dot(h, w_ref[:, m * ATT_W:(m + 1) * ATT_W])
    return [_dot(h, w_ref[:, N_ATT + j * BRANCH_W:N_ATT + (j + 1) * BRANCH_W]) for j in range(5)]


def _inproj_prompt_kernel(x_ref, ln_ref, w_ref, conv_ref, ws_ref, bst_ref,
                          h_ref, zatt_ref, oc_ref, od_ref, newc_ref, carry_ref, *, tm, tiles_per_seq):
    i = pl.program_id(0)
    gate_b, gate_c, x_c, u_d, v_d = _project(x_ref, ln_ref, w_ref, h_ref, zatt_ref)

    @pl.when(i % tiles_per_seq == 0)
    def _():
        carry_ref[...] = jnp.zeros_like(carry_ref)

    zc = gate_c * x_c
    carry = carry_ref[...]
    row = lax.broadcasted_iota(jnp.int32, zc.shape, 0)
    z1 = jnp.where(row == 0, carry[1:2], pltpu.roll(zc, 1, 0))
    z2 = jnp.where(row == 0, carry[0:1], jnp.where(row == 1, carry[1:2], pltpu.roll(zc, 2, 0)))
    conv = conv_ref[0:1] * z2 + conv_ref[1:2] * z1 + conv_ref[2:3] * zc
    oc_ref[...] = gate_b * conv
    tail = zc[tm - 2:tm]
    carry_ref[...] = tail
    newc_ref[...] = tail

    tri = (lax.broadcasted_iota(jnp.int32, (CHUNK, CHUNK), 0)
           >= lax.broadcasted_iota(jnp.int32, (CHUNK, CHUNK), 1))
    wm = [jnp.where(tri, ws_ref[g], 0.0).astype(BF16) for g in range(N_SG)]
    vb = v_d.astype(BF16)
    for c in range(tm // CHUNK):
        rows = slice(c * CHUNK, (c + 1) * CHUNK)
        sv = jnp.concatenate(
            [_dot(wm[g], vb[rows, g * HEAD_DIM:(g + 1) * HEAD_DIM]) for g in range(N_SG)], axis=1)
        od_ref[rows, :] = u_d[rows] * (sv + bst_ref[...])


def _inproj_sample_kernel(x_ref, ln_ref, w_ref, conv_ref, s0_ref, s1_ref, w00_ref, b0_ref,
                          h_ref, zatt_ref, oc_ref, od_ref, zc_ref, vd_ref):
    gate_b, gate_c, x_c, u_d, v_d = _project(x_ref, ln_ref, w_ref, h_ref, zatt_ref)
    zc = gate_c * x_c
    conv = conv_ref[0:1] * s0_ref[...] + conv_ref[1:2] * s1_ref[...] + conv_ref[2:3] * zc
    oc_ref[...] = gate_b * conv
    zc_ref[...] = zc
    od_ref[...] = u_d * (v_d * w00_ref[...] + b0_ref[...])
    vd_ref[...] = v_d


def _inproj_prompt(x2d, ln, w_bf, conv, ws, bst, *, seq, tm):
    m_rows = x2d.shape[0]
    assert m_rows % seq == 0 and seq % tm == 0 and tm % CHUNK == 0
    tps = seq // tm
    row = lambda w: pl.BlockSpec((tm, w), lambda i: (i, 0))
    return pl.pallas_call(
        functools.partial(_inproj_prompt_kernel, tm=tm, tiles_per_seq=tps),
        grid=(m_rows // tm,),
        in_specs=[row(D_MODEL), _const_spec((1, D_MODEL)), _const_spec((D_MODEL, N_IN)),
                  _const_spec((CONV_W, BRANCH_W)), _const_spec((N_SG, CHUNK, CHUNK)),
                  _const_spec((CHUNK, BRANCH_W))],
        out_specs=[row(D_MODEL), row(N_ATT), row(BRANCH_W), row(BRANCH_W),
                   pl.BlockSpec((None, CONV_W - 1, BRANCH_W), lambda i: (i // tps, 0, 0))],
        out_shape=[jax.ShapeDtypeStruct((m_rows, D_MODEL), BF16),
                   jax.ShapeDtypeStruct((m_rows, N_ATT), F32),
                   jax.ShapeDtypeStruct((m_rows, BRANCH_W), F32),
                   jax.ShapeDtypeStruct((m_rows, BRANCH_W), F32),
                   jax.ShapeDtypeStruct((m_rows // seq, CONV_W - 1, BRANCH_W), F32)],
        scratch_shapes=[pltpu.VMEM((CONV_W - 1, BRANCH_W), F32)],
        compiler_params=_params(1),
        name="inproj_prompt",
    )(x2d, ln, w_bf, conv, ws, bst)


def _inproj_sample(x2d, ln, w_bf, conv, s0, s1, w00, b0):
    n = x2d.shape[0]
    full = lambda w: _const_spec((n, w))
    return pl.pallas_call(
        _inproj_sample_kernel,
        grid=(1,),
        in_specs=[full(D_MODEL), _const_spec((1, D_MODEL)), _const_spec((D_MODEL, N_IN)),
                  _const_spec((CONV_W, BRANCH_W)), full(BRANCH_W), full(BRANCH_W),
                  _const_spec((1, BRANCH_W)), _const_spec((1, BRANCH_W))],
        out_specs=[full(D_MODEL), full(N_ATT)] + [full(BRANCH_W)] * 4,
        out_shape=[jax.ShapeDtypeStruct((n, D_MODEL), BF16), jax.ShapeDtypeStruct((n, N_ATT), F32)]
                  + [jax.ShapeDtypeStruct((n, BRANCH_W), F32)] * 4,
        compiler_params=_params(1),
        name="inproj_sample",
    )(x2d, ln, w_bf, conv, s0, s1, w00, b0)


def _segment_ones(n):
    r = lax.broadcasted_iota(jnp.int32, (n, n), 0) // HEAD_DIM
    c = lax.broadcasted_iota(jnp.int32, (n, n), 1) // HEAD_DIM
    return (r == c).astype(BF16)


def _head_norm_rope(x, g, ones, cos, sin_a, sin_b):
    x2 = x * x
    hi = x2.astype(BF16)
    lo = (x2 - hi.astype(F32)).astype(BF16)
    ss = _dot(hi, ones) + _dot(lo, ones)
    xn = x * lax.rsqrt(ss * (1.0 / HEAD_DIM) + EPS) * g
    n = x.shape[-1]
    return xn * cos + pltpu.roll(xn, n - ROT_HALF, 1) * sin_a + pltpu.roll(xn, ROT_HALF, 1) * sin_b


def _tile_lanes(t, reps):
    return jnp.concatenate([t] * reps, axis=1) if reps > 1 else t


def _load_lanes(ref, rows):
    return jnp.concatenate([ref[c, rows, :] for c in range(ref.shape[0])], axis=1)


def _store_lanes(ref, rows, val):
    for c in range(ref.shape[0]):
        ref[c, rows, :] = val[:, c * 128:(c + 1) * 128]


def _band_attention(d, with_sink, first_group, seq, sink_ref, qs_ref, kvs_ref, oa_ref, obs_ref, lse_ref):
    nblk = seq // (BAND * d)
    has_prev = nblk > 1
    nk = 2 * BAND if has_prev else BAND
    shift = nblk.bit_length() - 1
    rows3 = N_GROUP * BAND
    qi = lax.broadcasted_iota(jnp.int32, (rows3, nk), 0) % BAND
    kj = lax.broadcasted_iota(jnp.int32, (rows3, nk), 1)
    grp = lax.broadcasted_iota(jnp.int32, (rows3, 1), 0) // BAND

    def block(it, carry):
        r = it >> shift
        i = it & (nblk - 1)
        start = r + i * (BAND * d) if d > 1 else pl.multiple_of(it * BAND, BAND)
        rows = pl.ds(start, BAND, stride=d) if d > 1 else pl.ds(start, BAND)
        q = _load_lanes(qs_ref, rows)
        k_cur, v_cur = kvs_ref[0, rows, :], kvs_ref[1, rows, :]
        if has_prev:
            pstart = jnp.where(i == 0, start, start - BAND * d)
            if d == 1:
                pstart = pl.multiple_of(pstart, BAND)
            prows = pl.ds(pstart, BAND, stride=d) if d > 1 else pl.ds(pstart, BAND)
            k_all = jnp.concatenate([kvs_ref[0, prows, :], k_cur], axis=0)
            v_all = jnp.concatenate([kvs_ref[1, prows, :], v_cur], axis=0)
            lo = jnp.where(i == 0, BAND, 0)
            valid = (kj >= jnp.maximum(qi, lo)) & (kj <= qi + BAND)
        else:
            k_all, v_all = k_cur, v_cur
            valid = kj <= qi
        outs, lses = [], []
        for kvh in range(N_KV):
            qst = jnp.concatenate(
                [q[:, (N_GROUP * kvh + g) * HEAD_DIM:(N_GROUP * kvh + g + 1) * HEAD_DIM]
                 for g in range(N_GROUP)], axis=0).astype(BF16)
            kb = k_all[:, kvh * HEAD_DIM:(kvh + 1) * HEAD_DIM].astype(BF16)
            vb = v_all[:, kvh * HEAD_DIM:(kvh + 1) * HEAD_DIM].astype(BF16)
            s = lax.dot_general(qst, kb, (((1,), (1,)), ((), ())), preferred_element_type=F32)
            s = jnp.where(valid, s, NEG)
            mx = jnp.max(s, axis=1, keepdims=True)
            if with_sink:
                sk = jnp.where(grp == 0, sink_ref[N_GROUP * kvh],
                               jnp.where(grp == 1, sink_ref[N_GROUP * kvh + 1], sink_ref[N_GROUP * kvh + 2]))
                mx = jnp.maximum(mx, sk)
            p = jnp.exp(s - mx)
            l = jnp.sum(p, axis=1, keepdims=True)
            if with_sink:
                l = l + jnp.exp(sk - mx)
            o = _dot(p.astype(BF16), vb) / l
            lse = mx + jnp.log(l)
            for g in range(N_GROUP):
                outs.append(o[g * BAND:(g + 1) * BAND])
                lses.append(jnp.broadcast_to(lse[g * BAND:(g + 1) * BAND], (BAND, HEAD_DIM)))
        o_blk = jnp.concatenate(outs, axis=1)
        if with_sink:
            oa_ref[rows, :] = o_blk
        elif first_group:
            _store_lanes(obs_ref, rows, o_blk)
            _store_lanes(lse_ref, rows, jnp.concatenate(lses, axis=1))
        else:
            lse_new = jnp.concatenate(lses, axis=1)
            lse_old = _load_lanes(lse_ref, rows)
            top = jnp.maximum(lse_old, lse_new)
            w_old = jnp.exp(lse_old - top)
            w_new = jnp.exp(lse_new - top)
            den = w_old + w_new
            _store_lanes(obs_ref, rows, (w_old * _load_lanes(obs_ref, rows) + w_new * o_blk) / den)
            _store_lanes(lse_ref, rows, top + jnp.log(den))
        return carry

    lax.fori_loop(0, seq // BAND, block, 0)


def _attn_prompt_kernel(sink_ref, z_ref, gq_ref, gk_ref, cos_ref, sa_ref, sb_ref,
                        oa_ref, ob_ref, kvs_ref, qs_ref, obs_ref, lse_ref, *, seq, rc):
    m = pl.program_id(1)
    ones_q = _segment_ones(QW)
    ones_k = _segment_ones(KVW)

    def prep(c, carry):
        rows = pl.ds(pl.multiple_of(c * rc, rc), rc)
        cos, sa, sb = cos_ref[rows, :], sa_ref[rows, :], sb_ref[rows, :]
        reps = QW // KVW
        qn = _head_norm_rope(z_ref[rows, 0:QW], gq_ref[...], ones_q,
                             _tile_lanes(cos, reps), _tile_lanes(sa, reps), _tile_lanes(sb, reps))
        _store_lanes(qs_ref, rows, qn * (1.0 / math.sqrt(HEAD_DIM)))
        kvs_ref[0, rows, :] = _head_norm_rope(z_ref[rows, QW:QW + KVW], gk_ref[...], ones_k, cos, sa, sb)
        kvs_ref[1, rows, :] = z_ref[rows, QW + KVW:ATT_W]
        return carry

    lax.fori_loop(0, seq // rc, prep, 0)

    for idx, (_, d) in enumerate(MIXERS):
        @pl.when(m == idx)
        def _(idx=idx, d=d):
            _band_attention(d, idx == 0, idx == 1, seq, sink_ref, qs_ref, kvs_ref, oa_ref, obs_ref, lse_ref)

    @pl.when(m == N_MIX - 1)
    def _():
        ob_ref[...] = _load_lanes(obs_ref, slice(None))


def _attn_prompt(sink, z3, gq, gk, cos, sa, sb):
    nb, seq, _ = z3.shape
    assert all(seq % (BAND * d) == 0 for _, d in MIXERS)
    tab = _const_spec((seq, KVW))
    slabs = pltpu.VMEM((QW // 128, seq, 128), F32)
    return pl.pallas_call(
        functools.partial(_attn_prompt_kernel, seq=seq, rc=256),
        grid=(nb, N_MIX),
        in_specs=[pl.BlockSpec(memory_space=pltpu.SMEM),
                  pl.BlockSpec((None, seq, ATT_W), lambda b, m: (b, 0, m)),
                  pl.BlockSpec((None, 1, QW), lambda b, m: (m, 0, 0)),
                  pl.BlockSpec((None, 1, KVW), lambda b, m: (m, 0, 0)),
                  tab, tab, tab],
        out_specs=[pl.BlockSpec((None, seq, QW), lambda b, m: (b, 0, 0)),
                   pl.BlockSpec((None, seq, QW), lambda b, m: (b, 0, 0)),
                   pl.BlockSpec((None, None, 2, seq, KVW), lambda b, m: (m, b, 0, 0, 0))],
        out_shape=[jax.ShapeDtypeStruct((nb, seq, QW), F32),
                   jax.ShapeDtypeStruct((nb, seq, QW), F32),
                   jax.ShapeDtypeStruct((N_MIX, nb, 2, seq, KVW), F32)],
        scratch_shapes=[slabs, slabs, slabs],
        compiler_params=_params(2),
        name="attn_prompt",
    )(sink, z3, gq, gk, cos, sa, sb)


def _column_norm_rope(x, g, cos, sin):
    ss = jnp.sum(x * x, axis=0, keepdims=True)
    xn = x * lax.rsqrt(ss * (1.0 / HEAD_DIM) + EPS) * g
    x1, x2 = xn[0:ROT_HALF], xn[ROT_HALF:ROT_DIM]
    return jnp.concatenate([x1 * cos - x2 * sin, x2 * cos + x1 * sin, xn[ROT_DIM:]], axis=0)


def _decode_kernel(sink_ref, zt_ref, gq_ref, gk_ref, cos_ref, sin_ref,
                   ca_ref, c1_ref, c2_ref, c3_ref, *rest, bb, n_alias):
    rest = rest[n_alias:]
    oat_ref, obt_ref, na_ref, n1_ref, n2_ref, n3_ref, qn_ref, kn_ref = rest
    i = pl.program_id(0)
    nbatch = zt_ref.shape[1]
    cache_refs = (ca_ref, c1_ref, c2_ref, c3_ref)
    new_refs = (na_ref, n1_ref, n2_ref, n3_ref)

    @pl.when(i == 0)
    def _():
        cos, sin = cos_ref[...], sin_ref[...]
        for m in range(N_MIX):
            for h in range(N_HEADS):
                r = m * ATT_W + h * HEAD_DIM
                o = m * QW + h * HEAD_DIM
                qn_ref[o:o + HEAD_DIM, :] = _column_norm_rope(
                    zt_ref[r:r + HEAD_DIM, :], gq_ref[m * HEAD_DIM:(m + 1) * HEAD_DIM, :], cos, sin
                ) * (1.0 / math.sqrt(HEAD_DIM))
            for kvh in range(N_KV):
                r = m * ATT_W + QW + kvh * HEAD_DIM
                o = m * KVW + kvh * HEAD_DIM
                kn_ref[o:o + HEAD_DIM, :] = _column_norm_rope(
                    zt_ref[r:r + HEAD_DIM, :], gk_ref[m * HEAD_DIM:(m + 1) * HEAD_DIM, :], cos, sin)
        oat_ref[...] = jnp.zeros_like(oat_ref)
        obt_ref[...] = jnp.zeros_like(obt_ref)

    lane_b = lax.broadcasted_iota(jnp.int32, (HEAD_DIM, nbatch), 1)

    def per_b(b, carry):
        sel = lane_b == (i * bb + b)

        def column(tile):
            return jnp.sum(jnp.where(sel, tile, 0.0), axis=1, keepdims=True)

        def put(ref, row0, col):
            ref[row0:row0 + HEAD_DIM, :] = jnp.where(sel, col, ref[row0:row0 + HEAD_DIM, :])

        for kvh in range(N_KV):
            run = [None] * N_GROUP
            for m, (win, d) in enumerate(MIXERS):
                cref, nref = cache_refs[m], new_refs[m]
                kt = cref[b, 0, kvh]
                vt = cref[b, 1, kvh]
                kn = column(kn_ref[m * KVW + kvh * HEAD_DIM:m * KVW + (kvh + 1) * HEAD_DIM, :])
                r = m * ATT_W + QW + KVW + kvh * HEAD_DIM
                vn = column(zt_ref[r:r + HEAD_DIM, :])
                lane = lax.broadcasted_iota(jnp.int32, (HEAD_DIM, win), 1)
                last = lane == win - 1
                nref[b, 0, kvh] = jnp.where(last, kn, pltpu.roll(kt, win - 1, 1))
                nref[b, 1, kvh] = jnp.where(last, vn, pltpu.roll(vt, win - 1, 1))
                keep = (lax.broadcasted_iota(jnp.int32, (1, win), 1) % d) == 0
                for g in range(N_GROUP):
                    h = N_GROUP * kvh + g
                    qc = column(qn_ref[m * QW + h * HEAD_DIM:m * QW + (h + 1) * HEAD_DIM, :])
                    s = jnp.sum(qc * kt, axis=0, keepdims=True)
                    if d > 1:
                        s = jnp.where(keep, s, NEG)
                    s_new = jnp.sum(qc * kn, axis=0, keepdims=True)
                    mx = jnp.maximum(jnp.max(s, axis=1, keepdims=True), s_new)
                    if m == 0:
                        sk = sink_ref[h]
                        mx = jnp.maximum(mx, sk)
                    p = jnp.exp(s - mx)
                    p_new = jnp.exp(s_new - mx)
                    l = jnp.sum(p, axis=1, keepdims=True) + p_new
                    acc = jnp.sum(p * vt, axis=1, keepdims=True) + p_new * vn
                    if m == 0:
                        l = l + jnp.exp(sk - mx)
                        put(oat_ref, h * HEAD_DIM, acc / l)
                    elif run[g] is None:
                        run[g] = (mx, l, acc)
                    else:
                        mx0, l0, acc0 = run[g]
                        top = jnp.maximum(mx0, mx)
                        a0 = jnp.exp(mx0 - top)
                        a1 = jnp.exp(mx - top)
                        run[g] = (top, a0 * l0 + a1 * l, a0 * acc0 + a1 * acc)
            for g in range(N_GROUP):
                _, l, acc = run[g]
                put(obt_ref, (N_GROUP * kvh + g) * HEAD_DIM, acc / l)
        return carry

    lax.fori_loop(0, bb, per_b, 0)


def _decode(layer, sink, zt, gq_col, gk_col, cos_col, sin_col, caches_t, prev_out, *, bb):
    nbatch = zt.shape[1]
    assert nbatch % bb == 0
    cache_specs = [pl.BlockSpec((None, bb) + c.shape[2:], lambda i: (layer, i, 0, 0, 0, 0)) for c in caches_t]
    n_alias = 0 if prev_out is None else len(prev_out)
    alias_specs = [pl.BlockSpec(memory_space=pl.ANY)] * n_alias
    n_in = 6 + len(caches_t)
    outs = pl.pallas_call(
        functools.partial(_decode_kernel, bb=bb, n_alias=n_alias),
        grid=(nbatch // bb,),
        in_specs=[pl.BlockSpec(memory_space=pltpu.SMEM), _const_spec(zt.shape),
                  _const_spec(gq_col.shape), _const_spec(gk_col.shape),
                  _const_spec(cos_col.shape), _const_spec(sin_col.shape)] + cache_specs + alias_specs,
        out_specs=[_const_spec((QW, nbatch)), _const_spec((QW, nbatch))] + cache_specs,
        out_shape=[jax.ShapeDtypeStruct((QW, nbatch), F32), jax.ShapeDtypeStruct((QW, nbatch), F32)]
                  + [jax.ShapeDtypeStruct(c.shape, F32) for c in caches_t],
        scratch_shapes=[pltpu.VMEM((N_MIX * QW, nbatch), F32), pltpu.VMEM((N_MIX * KVW, nbatch), F32)],
        input_output_aliases={n_in + j: 2 + j for j in range(n_alias)},
        compiler_params=_params(1),
        name="decode_attn_cache",
    )(sink, zt, gq_col, gk_col, cos_col, sin_col, *caches_t, *(prev_out or ()))
    return outs[0], outs[1], tuple(outs[2:])


def _merge_kernel(x_ref, h_ref, oa_ref, ob_ref, oc_ref, od_ref, wg_ref, bg_ref, wb_ref, wo_ref, out_ref):
    h = h_ref[...]
    merged = None
    for n, br_ref in enumerate((oa_ref, ob_ref, oc_ref, od_ref)):
        gate = jax.nn.sigmoid(_dot(h, wg_ref[n]) + bg_ref[n:n + 1])
        t = gate * _dot(br_ref[...].astype(BF16), wb_ref[n])
        merged = t if merged is None else merged + t
    out_ref[...] = x_ref[...] + _dot(merged.astype(BF16), wo_ref[...])


def _merge(x2d, h, oa, ob, oc, od, wg, bg, wb, wo, *, tm):
    m_rows = x2d.shape[0]
    assert m_rows % tm == 0
    row = lambda w: pl.BlockSpec((tm, w), lambda i: (i, 0))
    return pl.pallas_call(
        _merge_kernel,
        grid=(m_rows // tm,),
        in_specs=[row(D_MODEL), row(D_MODEL)] + [row(BRANCH_W)] * N_BRANCH
                 + [_const_spec(wg.shape), _const_spec(bg.shape), _const_spec(wb.shape), _const_spec(wo.shape)],
        out_specs=row(D_MODEL),
        out_shape=jax.ShapeDtypeStruct((m_rows, D_MODEL), F32),
        compiler_params=_params(1),
        name="branch_merge",
    )(x2d, h, oa, ob, oc, od, wg, bg, wb, wo)


def _ffn_kernel(x_ref, ln_ref, wi_ref, wo_ref, out_ref, act_ref):
    x = x_ref[...]
    h = _rms(x, ln_ref[...]).astype(BF16)
    for j in range(D_FF // FF_CHUNK):
        cols = slice(j * FF_CHUNK, (j + 1) * FF_CHUNK)
        gt = _dot(h, wi_ref[:, cols])
        up = _dot(h, wi_ref[:, D_FF + j * FF_CHUNK:D_FF + (j + 1) * FF_CHUNK])
        act_ref[:, cols] = (gt * jax.nn.sigmoid(gt) * up).astype(BF16)
    out_ref[...] = x + _dot(act_ref[...], wo_ref[...])


def _ffn(x2d, ln, wi, wo, *, tm):
    m_rows = x2d.shape[0]
    assert m_rows % tm == 0 and D_FF % FF_CHUNK == 0
    row = pl.BlockSpec((tm, D_MODEL), lambda i: (i, 0))
    return pl.pallas_call(
        _ffn_kernel,
        grid=(m_rows // tm,),
        in_specs=[row, _const_spec(ln.shape), _const_spec(wi.shape), _const_spec(wo.shape)],
        out_specs=row,
        out_shape=jax.ShapeDtypeStruct((m_rows, D_MODEL), F32),
        scratch_shapes=[pltpu.VMEM((tm, D_FF), BF16)],
        compiler_params=_params(1),
        name="swiglu_ffn",
    )(x2d, ln, wi, wo)


def _rope_angles(pos):
    inv_freq = ROPE_THETA ** (-2.0 * jnp.arange(ROT_HALF, dtype=F32) / ROT_DIM)
    ang = pos[:, None] * inv_freq[None, :]
    return jnp.cos(ang), jnp.sin(ang)


def _rope_row_tables(pos):
    cos, sin = _rope_angles(pos)
    n = pos.shape[0]
    zero = jnp.zeros((n, ROT_HALF), F32)
    rest0 = jnp.zeros((n, HEAD_DIM - ROT_DIM), F32)
    c = jnp.concatenate([cos, cos, jnp.ones((n, HEAD_DIM - ROT_DIM), F32)], axis=1)
    sa = jnp.concatenate([-sin, zero, rest0], axis=1)
    sb = jnp.concatenate([zero, sin, rest0], axis=1)
    return tuple(jnp.tile(t, (1, N_KV)) for t in (c, sa, sb))


def kernel(x_prompt, x_sample, cache_a, cache_b1, cache_b2, cache_b3, state_c, ln1, w_in, q_norm_a, k_norm_a,
           sink_a, q_norm_b, k_norm_b, conv_c, ws_d, bs_d, w_gate, b_gate, w_branch, w_o, ln2, w_ffn_in,
           w_ffn_out):
    bp, sp, _ = x_prompt.shape
    bs_n, ds, _ = x_sample.shape
    depth = ln1.shape[0]
    assert ds == 1 and sp >= CONV_W - 1
    caches = (cache_a, cache_b1, cache_b2, cache_b3)
    for c, (win, d) in zip(caches, MIXERS):
        assert c.shape[3] == win and win // d == BAND

    cos_p, sa_p, sb_p = _rope_row_tables(jnp.arange(sp, dtype=F32))
    cos_s, sin_s = _rope_angles(PAST_LEN + jnp.arange(ds, dtype=F32))
    cos_col, sin_col = cos_s.reshape(ROT_HALF, 1), sin_s.reshape(ROT_HALF, 1)
    caches_t = tuple(jnp.transpose(c, (0, 1, 2, 4, 5, 3)) for c in caches)

    y_p = x_prompt.reshape(bp * sp, D_MODEL)
    y_s = x_sample.reshape(bs_n, D_MODEL)
    kv_p, c_p, c_s, d_s = [], [], [], []
    new_t = None
    for l in range(depth):
        w_in_b = w_in[l].astype(BF16)
        wg_b, wb_b, wo_b = w_gate[l].astype(BF16), w_branch[l].astype(BF16), w_o[l].astype(BF16)
        wfi_b, wfo_b = w_ffn_in[l].astype(BF16), w_ffn_out[l].astype(BF16)
        ln1_l, ln2_l = ln1[l][None], ln2[l][None]
        gq_all = jnp.concatenate([q_norm_a[l][None], q_norm_b[l]], axis=0)
        gk_all = jnp.concatenate([k_norm_a[l][None], k_norm_b[l]], axis=0)
        bst = jnp.repeat(bs_d[l].T, HEAD_DIM, axis=1)

        h, zatt, oc, od, newc = _inproj_prompt(y_p, ln1_l, w_in_b, conv_c[l], ws_d[l], bst, seq=sp, tm=ROW_TILE)
        oa, ob, kv = _attn_prompt(sink_a[l], zatt.reshape(bp, sp, N_ATT),
                                  jnp.tile(gq_all, (1, N_HEADS))[:, None], jnp.tile(gk_all, (1, N_KV))[:, None],
                                  cos_p, sa_p, sb_p)
        x1 = _merge(y_p, h, oa.reshape(bp * sp, QW), ob.reshape(bp * sp, QW), oc, od,
                    wg_b, b_gate[l], wb_b, wo_b, tm=ROW_TILE)
        y_p = _ffn(x1, ln2_l, wfi_b, wfo_b, tm=ROW_TILE)
        kv_p.append(kv)
        c_p.append(newc)

        s0, s1 = state_c[l][:, 0], state_c[l][:, 1]
        w00 = jnp.repeat(ws_d[l][:, 0, 0], HEAD_DIM)[None]
        b0 = jnp.repeat(bs_d[l][:, 0], HEAD_DIM)[None]
        h, zatt, oc, od, zc, vd = _inproj_sample(y_s, ln1_l, w_in_b, conv_c[l], s0, s1, w00, b0)
        oat, obt, new_t = _decode(l, sink_a[l], zatt.T, gq_all.reshape(-1, 1), gk_all.reshape(-1, 1),
                                  cos_col, sin_col, caches_t, new_t, bb=DEC_BLOCK)
        x1 = _merge(y_s, h, oat.T, obt.T, oc, od, wg_b, b_gate[l], wb_b, wo_b, tm=bs_n)
        y_s = _ffn(x1, ln2_l, wfi_b, wfo_b, tm=bs_n)
        c_s.append(jnp.stack([s1, zc], axis=1))
        d_s.append(vd[:, None])

    def prompt_cache(m):
        win = min(MIXERS[m][0], sp)
        kvm = jnp.stack([kv[m] for kv in kv_p])[:, :, :, sp - win:]
        return kvm.reshape(depth, bp, 2, win, N_KV, HEAD_DIM)

    sample_cache = [jnp.transpose(t, (0, 1, 2, 5, 3, 4)) for t in new_t]
    return (y_p.reshape(bp, sp, D_MODEL), y_s.reshape(bs_n, ds, D_MODEL),
            prompt_cache(0), sample_cache[0], prompt_cache(1), sample_cache[1],
            prompt_cache(2), sample_cache[2], prompt_cache(3), sample_cache[3],
            jnp.stack(c_p), jnp.stack(c_s), jnp.stack(d_s))
```

```python
import functools
import math

import jax
import jax.numpy as jnp
from jax import lax
from jax.experimental import pallas as pl
from jax.experimental.pallas import tpu as pltpu

F32 = jnp.float32
BF16 = jnp.bfloat16

D_MODEL = 1024
HEAD_DIM = 64
ROT_DIM = HEAD_DIM // 4
ROT_HALF = ROT_DIM // 2
ROPE_THETA = 500000.0
BRANCH_W = 384
N_HEADS = BRANCH_W // HEAD_DIM
N_KV = 2
N_GROUP = N_HEADS // N_KV
QW = N_HEADS * HEAD_DIM
KVW = N_KV * HEAD_DIM
ATT_W = QW + 2 * KVW
WIN_A = 128
DIL_GROUPS = ((128, 1), (512, 4), (2048, 16))
MIXERS = ((WIN_A, 1),) + DIL_GROUPS
N_MIX = len(MIXERS)
BAND = 128
LANES = 128
SLOT_HEADS = (0, 3, 1, 4, 2, 5)
SLOT_OF_HEAD = tuple(SLOT_HEADS.index(h) for h in range(6))
CONV_W = 3
CHUNK = 128
N_SG = 6
N_BRANCH = 4
N_ATT = ATT_W * N_MIX
N_IN = N_ATT + 5 * BRANCH_W
D_FF = 2816
FF_CHUNK = 256
PAST_LEN = 16384
EPS = 1e-6
NEG = -1e30

VMEM_LIMIT_V7X = 56 * 1024 * 1024
ROW_TILE = 512
DEC_BLOCK = 2
BLOCK_UNROLL = 4


def _params(n_axes, vmem=VMEM_LIMIT_V7X):
    return pltpu.CompilerParams(dimension_semantics=("arbitrary",) * n_axes, vmem_limit_bytes=vmem)


def _const_spec(shape):
    nd = len(shape)
    return pl.BlockSpec(shape, lambda *_: (0,) * nd)


def _rms(x, g):
    return x * lax.rsqrt(jnp.mean(x * x, axis=-1, keepdims=True) + EPS) * g


def _dot(a, b):
    return jnp.dot(a, b, preferred_element_type=F32)


def _project(x_ref, ln_ref, w_ref, h_ref, zatt_ref, slabs):
    h = _rms(x_ref[...], ln_ref[...]).astype(BF16)
    h_ref[...] = h
    per = ATT_W // LANES
    for m in range(N_MIX):
        z = _dot(h, w_ref[:, m * ATT_W:(m + 1) * ATT_W])
        if slabs:
            for c in range(per):
                zatt_ref[m * per + c] = z[:, c * LANES:(c + 1) * LANES]
        else:
            zatt_ref[:, m * ATT_W:(m + 1) * ATT_W] = z
    return [_dot(h, w_ref[:, N_ATT + j * BRANCH_W:N_ATT + (j + 1) * BRANCH_W]) for j in range(5)]


def _inproj_prompt_kernel(x_ref, ln_ref, w_ref, conv_ref, ws_ref, bst_ref,
                          h_ref, zatt_ref, oc_ref, od_ref, newc_ref, carry_ref, *, tm, tiles_per_seq):
    i = pl.program_id(0)
    gate_b, gate_c, x_c, u_d, v_d = _project(x_ref, ln_ref, w_ref, h_ref, zatt_ref, True)

    @pl.when(i % tiles_per_seq == 0)
    def _():
        carry_ref[...] = jnp.zeros_like(carry_ref)

    zc = gate_c * x_c
    carry = carry_ref[...]
    row = lax.broadcasted_iota(jnp.int32, zc.shape, 0)
    z1 = jnp.where(row == 0, carry[1:2], pltpu.roll(zc, 1, 0))
    z2 = jnp.where(row == 0, carry[0:1], jnp.where(row == 1, carry[1:2], pltpu.roll(zc, 2, 0)))
    conv = conv_ref[0:1] * z2 + conv_ref[1:2] * z1 + conv_ref[2:3] * zc
    oc_ref[...] = gate_b * conv
    tail = zc[tm - 2:tm]
    carry_ref[...] = tail
    newc_ref[...] = tail

    tri = (lax.broadcasted_iota(jnp.int32, (CHUNK, CHUNK), 0)
           >= lax.broadcasted_iota(jnp.int32, (CHUNK, CHUNK), 1))
    wm = [jnp.where(tri, ws_ref[g], 0.0).astype(BF16) for g in range(N_SG)]
    vb = v_d.astype(BF16)
    for c in range(tm // CHUNK):
        rows = slice(c * CHUNK, (c + 1) * CHUNK)
        sv = jnp.concatenate(
            [_dot(wm[g], vb[rows, g * HEAD_DIM:(g + 1) * HEAD_DIM]) for g in range(N_SG)], axis=1)
        od_ref[rows, :] = u_d[rows] * (sv + bst_ref[...])


def _inproj_sample_kernel(x_ref, ln_ref, w_ref, conv_ref, s0_ref, s1_ref, w00_ref, b0_ref,
                          h_ref, zatt_ref, oc_ref, od_ref, zc_ref, vd_ref):
    gate_b, gate_c, x_c, u_d, v_d = _project(x_ref, ln_ref, w_ref, h_ref, zatt_ref, False)
    zc = gate_c * x_c
    conv = conv_ref[0:1] * s0_ref[...] + conv_ref[1:2] * s1_ref[...] + conv_ref[2:3] * zc
    oc_ref[...] = gate_b * conv
    zc_ref[...] = zc
    od_ref[...] = u_d * (v_d * w00_ref[...] + b0_ref[...])
    vd_ref[...] = v_d


def _inproj_prompt(x2d, ln, w_bf, conv, ws, bst, *, seq, tm):
    m_rows = x2d.shape[0]
    assert m_rows % seq == 0 and seq % tm == 0 and tm % CHUNK == 0
    tps = seq // tm
    row = lambda w: pl.BlockSpec((tm, w), lambda i: (i, 0))
    return pl.pallas_call(
        functools.partial(_inproj_prompt_kernel, tm=tm, tiles_per_seq=tps),
        grid=(m_rows // tm,),
        in_specs=[row(D_MODEL), _const_spec((1, D_MODEL)), _const_spec((D_MODEL, N_IN)),
                  _const_spec((CONV_W, BRANCH_W)), _const_spec((N_SG, CHUNK, CHUNK)),
                  _const_spec((CHUNK, BRANCH_W))],
        out_specs=[row(D_MODEL), pl.BlockSpec((N_ATT // LANES, tm, LANES), lambda i: (0, i, 0)),
                   row(BRANCH_W), row(BRANCH_W),
                   pl.BlockSpec((None, CONV_W - 1, BRANCH_W), lambda i: (i // tps, 0, 0))],
        out_shape=[jax.ShapeDtypeStruct((m_rows, D_MODEL), BF16),
                   jax.ShapeDtypeStruct((N_ATT // LANES, m_rows, LANES), F32),
                   jax.ShapeDtypeStruct((m_rows, BRANCH_W), F32),
                   jax.ShapeDtypeStruct((m_rows, BRANCH_W), F32),
                   jax.ShapeDtypeStruct((m_rows // seq, CONV_W - 1, BRANCH_W), F32)],
        scratch_shapes=[pltpu.VMEM((CONV_W - 1, BRANCH_W), F32)],
        compiler_params=_params(1),
        name="inproj_prompt",
    )(x2d, ln, w_bf, conv, ws, bst)


def _inproj_sample(x2d, ln, w_bf, conv, s0, s1, w00, b0):
    n = x2d.shape[0]
    full = lambda w: _const_spec((n, w))
    return pl.pallas_call(
        _inproj_sample_kernel,
        grid=(1,),
        in_specs=[full(D_MODEL), _const_spec((1, D_MODEL)), _const_spec((D_MODEL, N_IN)),
                  _const_spec((CONV_W, BRANCH_W)), full(BRANCH_W), full(BRANCH_W),
                  _const_spec((1, BRANCH_W)), _const_spec((1, BRANCH_W))],
        out_specs=[full(D_MODEL), full(N_ATT)] + [full(BRANCH_W)] * 4,
        out_shape=[jax.ShapeDtypeStruct((n, D_MODEL), BF16), jax.ShapeDtypeStruct((n, N_ATT), F32)]
                  + [jax.ShapeDtypeStruct((n, BRANCH_W), F32)] * 4,
        compiler_params=_params(1),
        name="inproj_sample",
    )(x2d, ln, w_bf, conv, s0, s1, w00, b0)


def _segment_ones(n):
    r = lax.broadcasted_iota(jnp.int32, (n, n), 0) // HEAD_DIM
    c = lax.broadcasted_iota(jnp.int32, (n, n), 1) // HEAD_DIM
    return (r == c).astype(BF16)


def _head_norm_rope(xs, gains, ones, cos, sin_a, sin_b):
    n, rows = len(xs), xs[0].shape[0]
    sq = [x * x for x in xs]
    hi = [t.astype(BF16) for t in sq]
    lo = [(t - h.astype(F32)).astype(BF16) for t, h in zip(sq, hi)]
    ss = _dot(jnp.concatenate(hi + lo, axis=0), ones)
    out = []
    for c, (x, g) in enumerate(zip(xs, gains)):
        ssc = ss[c * rows:(c + 1) * rows] + ss[(n + c) * rows:(n + c + 1) * rows]
        xn = x * lax.rsqrt(ssc * (1.0 / HEAD_DIM) + EPS) * g
        out.append(xn * cos + pltpu.roll(xn, LANES - ROT_HALF, 1) * sin_a + pltpu.roll(xn, ROT_HALF, 1) * sin_b)
    return out


def _attn_mixer(d, idx, seq, sink_ref, z_ref, gq_ref, gk_ref, cos_ref, sa_ref, sb_ref,
                oa_ref, kv_ref, qs_ref, kt_ref, vs_ref, obs_ref, lse_ref, bias_ref):
    with_sink, first_group = idx == 0, idx == 1
    nblk = seq // (BAND * d)
    has_prev = nblk > 1
    shift = nblk.bit_length() - 1
    ones = _segment_ones(LANES)
    low = lax.broadcasted_iota(jnp.int32, (BAND, LANES), 1) < HEAD_DIM

    def natural_rows(it):
        if d == 1:
            return pl.ds(pl.multiple_of(it * BAND, BAND), BAND)
        return pl.ds((it >> shift) + (it & (nblk - 1)) * (BAND * d), BAND, stride=d)

    nb = seq // BAND
    nk = 2 * BAND if has_prev else BAND
    nq = QW // LANES
    ones_cols = jnp.ones((nk, LANES), BF16)

    def block_pos(b):
        return pl.ds(pl.multiple_of(b * BAND, BAND), BAND)

    def prev_block(b):
        return jnp.maximum(b - 1, 0)

    def prep_stage(it):
        rows = natural_rows(it)
        pos = block_pos(it)
        cos, sa, sb = cos_ref[rows, :], sa_ref[rows, :], sb_ref[rows, :]
        *qn, kn = _head_norm_rope([z_ref[c, rows, :] for c in range(nq + 1)],
                                  [gq_ref[...]] * nq + [gk_ref[...]], ones, cos, sa, sb)
        for c in range(nq):
            qc = (qn[c] * (1.0 / math.sqrt(HEAD_DIM))).astype(BF16)
            qs_ref[2 * c, pos, :] = jnp.where(low, qc, jnp.zeros_like(qc))
            qs_ref[2 * c + 1, pos, :] = jnp.where(low, jnp.zeros_like(qc), qc)
        v = z_ref[nq + 1, rows, :]
        kv_ref[0, rows, :] = kn
        kv_ref[1, rows, :] = v
        kt_ref[it] = kn.T.astype(BF16)
        vs_ref[pos, :] = v.astype(BF16)

    def score_stage(b):
        q6 = jnp.concatenate([qs_ref[j, block_pos(b), :] for j in range(N_HEADS)], axis=0)
        if has_prev:
            first = (b & (nblk - 1)) == 0
            kt = jnp.concatenate([kt_ref[prev_block(b)], kt_ref[b]], axis=1)
            bias = bias_ref[first.astype(jnp.int32)]
        else:
            kt = kt_ref[b]
            bias = bias_ref[1, :, BAND:]
        return _dot(q6, kt) + jnp.concatenate([bias] * N_HEADS, axis=0)

    def softmax_stage(s):
        mx = jnp.max(s, axis=1, keepdims=True)
        p = jnp.exp(s - mx).astype(BF16)
        return p, [jnp.where(low, mx[2 * c * BAND:(2 * c + 1) * BAND], mx[(2 * c + 1) * BAND:(2 * c + 2) * BAND])
                   for c in range(nq)]

    def value_stage(b, p):
        v = vs_ref[block_pos(b), :]
        if has_prev:
            v = jnp.concatenate([vs_ref[block_pos(prev_block(b)), :], v], axis=0)
        v_aug = jnp.concatenate([v, ones_cols], axis=1)
        return _dot(p, v_aug)

    def output_stage(b, r2, mxs):
        pos = block_pos(b)
        rows = natural_rows(b)
        for c in range(nq):
            ra, rb = slice(2 * c * BAND, (2 * c + 1) * BAND), slice((2 * c + 1) * BAND, (2 * c + 2) * BAND)
            dn = jnp.where(low, r2[ra, LANES:], r2[rb, LANES:])
            mx = mxs[c]
            if with_sink:
                dn = dn + jnp.exp(jnp.where(low, sink_ref[SLOT_HEADS[2 * c]], sink_ref[SLOT_HEADS[2 * c + 1]]) - mx)
            o = jnp.where(low, r2[ra, :LANES], r2[rb, :LANES]) / dn
            if with_sink:
                oa_ref[c, pos, :] = o
                continue
            lse = mx + jnp.log(dn)
            if first_group:
                obs_ref[c, rows, :] = o
                lse_ref[c, rows, :] = lse
            else:
                lse_old = lse_ref[c, rows, :]
                top = jnp.maximum(lse_old, lse)
                w_old = jnp.exp(lse_old - top)
                w_new = jnp.exp(lse - top)
                tot = w_old + w_new
                obs_ref[c, rows, :] = (w_old * obs_ref[c, rows, :] + w_new * o) / tot
                lse_ref[c, rows, :] = top + jnp.log(tot)

    def prep_step(b, carry):
        prep_stage(b)
        return carry

    def attn_step(b, carry):
        p, mxs = softmax_stage(score_stage(b))
        output_stage(b, value_stage(b, p), mxs)
        return carry

    lax.fori_loop(0, nb, prep_step, 0, unroll=BLOCK_UNROLL)
    lax.fori_loop(0, nb, attn_step, 0, unroll=BLOCK_UNROLL)


def _attn_prompt_kernel(sink_ref, z_ref, gq_ref, gk_ref, cos_ref, sa_ref, sb_ref,
                        oa_ref, ob_ref, kv_ref, qs_ref, kt_ref, vs_ref, obs_ref, lse_ref, bias_ref, *, seq):
    m = pl.program_id(1)
    qi = lax.broadcasted_iota(jnp.int32, (BAND, 2 * BAND), 0)
    kj = lax.broadcasted_iota(jnp.int32, (BAND, 2 * BAND), 1)
    in_band = kj <= qi + BAND
    bias_ref[0] = jnp.where(in_band & (kj >= qi), 0.0, NEG)
    bias_ref[1] = jnp.where(in_band & (kj >= BAND), 0.0, NEG)

    for idx, (_, d) in enumerate(MIXERS):
        @pl.when(m == idx)
        def _(idx=idx, d=d):
            _attn_mixer(d, idx, seq, sink_ref, z_ref, gq_ref, gk_ref, cos_ref, sa_ref, sb_ref,
                        oa_ref, kv_ref, qs_ref, kt_ref, vs_ref, obs_ref, lse_ref, bias_ref)

    @pl.when(m == N_MIX - 1)
    def _():
        ob_ref[...] = obs_ref[...]


def _attn_prompt(sink, z4, gq, gk, cos, sa, sb):
    _, nb, seq, _ = z4.shape
    assert all(seq % (BAND * d) == 0 for _, d in MIXERS)
    per = ATT_W // LANES
    nq = QW // LANES
    tab = _const_spec((seq, LANES))
    gain = pl.BlockSpec((None, 1, LANES), lambda b, m: (m, 0, 0))
    o_spec = pl.BlockSpec((None, nq, seq, LANES), lambda b, m: (b, 0, 0, 0))
    slabs = pltpu.VMEM((nq, seq, LANES), F32)
    return pl.pallas_call(
        functools.partial(_attn_prompt_kernel, seq=seq),
        grid=(nb, N_MIX),
        in_specs=[pl.BlockSpec(memory_space=pltpu.SMEM),
                  pl.BlockSpec((per, None, seq, LANES), lambda b, m: (m, b, 0, 0)),
                  gain, gain, tab, tab, tab],
        out_specs=[o_spec, o_spec,
                   pl.BlockSpec((None, None, 2, seq, LANES), lambda b, m: (m, b, 0, 0, 0))],
        out_shape=[jax.ShapeDtypeStruct((nb, nq, seq, LANES), F32),
                   jax.ShapeDtypeStruct((nb, nq, seq, LANES), F32),
                   jax.ShapeDtypeStruct((N_MIX, nb, 2, seq, LANES), F32)],
        scratch_shapes=[pltpu.VMEM((N_HEADS, seq, LANES), BF16),
                        pltpu.VMEM((seq // BAND, LANES, BAND), BF16),
                        pltpu.VMEM((seq, LANES), BF16),
                        slabs, slabs,
                        pltpu.VMEM((2, BAND, 2 * BAND), F32)],
        compiler_params=_params(2),
        name="attn_prompt",
    )(sink, z4, gq, gk, cos, sa, sb)


def _column_norm_rope(x, g, cos, sin):
    ss = jnp.sum(x * x, axis=0, keepdims=True)
    xn = x * lax.rsqrt(ss * (1.0 / HEAD_DIM) + EPS) * g
    x1, x2 = xn[0:ROT_HALF], xn[ROT_HALF:ROT_DIM]
    return jnp.concatenate([x1 * cos - x2 * sin, x2 * cos + x1 * sin, xn[ROT_DIM:]], axis=0)


def _decode_kernel(sink_ref, zt_ref, gq_ref, gk_ref, cos_ref, sin_ref,
                   ca_ref, c1_ref, c2_ref, c3_ref, *rest, bb, n_alias):
    rest = rest[n_alias:]
    oat_ref, obt_ref, na_ref, n1_ref, n2_ref, n3_ref, qn_ref, kn_ref = rest
    i = pl.program_id(0)
    nbatch = zt_ref.shape[1]
    cache_refs = (ca_ref, c1_ref, c2_ref, c3_ref)
    new_refs = (na_ref, n1_ref, n2_ref, n3_ref)

    @pl.when(i == 0)
    def _():
        cos, sin = cos_ref[...], sin_ref[...]
        for m in range(N_MIX):
            for h in range(N_HEADS):
                r = m * ATT_W + h * HEAD_DIM
                o = m * QW + h * HEAD_DIM
                qn_ref[o:o + HEAD_DIM, :] = _column_norm_rope(
                    zt_ref[r:r + HEAD_DIM, :], gq_ref[m * HEAD_DIM:(m + 1) * HEAD_DIM, :], cos, sin
                ) * (1.0 / math.sqrt(HEAD_DIM))
            for kvh in range(N_KV):
                r = m * ATT_W + QW + kvh * HEAD_DIM
                o = m * KVW + kvh * HEAD_DIM
                kn_ref[o:o + HEAD_DIM, :] = _column_norm_rope(
                    zt_ref[r:r + HEAD_DIM, :], gk_ref[m * HEAD_DIM:(m + 1) * HEAD_DIM, :], cos, sin)
        oat_ref[...] = jnp.zeros_like(oat_ref)
        obt_ref[...] = jnp.zeros_like(obt_ref)

    lane_b = lax.broadcasted_iota(jnp.int32, (HEAD_DIM, nbatch), 1)

    def per_b(b):
        sel = lane_b == (i * bb + b)

        def column(tile):
            return jnp.sum(jnp.where(sel, tile, 0.0), axis=1, keepdims=True)

        def put(ref, row0, col):
            ref[row0:row0 + HEAD_DIM, :] = jnp.where(sel, col, ref[row0:row0 + HEAD_DIM, :])

        for kvh in range(N_KV):
            run = [None] * N_GROUP
            for m, (win, d) in enumerate(MIXERS):
                cref, nref = cache_refs[m], new_refs[m]
                kt = cref[b, 0, kvh]
                vt = cref[b, 1, kvh]
                kn = column(kn_ref[m * KVW + kvh * HEAD_DIM:m * KVW + (kvh + 1) * HEAD_DIM, :])
                r = m * ATT_W + QW + KVW + kvh * HEAD_DIM
                vn = column(zt_ref[r:r + HEAD_DIM, :])
                lane = lax.broadcasted_iota(jnp.int32, (HEAD_DIM, win), 1)
                last = lane == win - 1
                nref[b, 0, kvh] = jnp.where(last, kn, pltpu.roll(kt, win - 1, 1))
                nref[b, 1, kvh] = jnp.where(last, vn, pltpu.roll(vt, win - 1, 1))
                keep = (lax.broadcasted_iota(jnp.int32, (1, win), 1) % d) == 0
                for g in range(N_GROUP):
                    h = N_GROUP * kvh + g
                    slot = SLOT_OF_HEAD[h]
                    qc = column(qn_ref[m * QW + slot * HEAD_DIM:m * QW + (slot + 1) * HEAD_DIM, :])
                    s = jnp.sum(qc * kt, axis=0, keepdims=True)
                    if d > 1:
                        s = jnp.where(keep, s, NEG)
                    s_new = jnp.sum(qc * kn, axis=0, keepdims=True)
                    mx = jnp.maximum(jnp.max(s, axis=1, keepdims=True), s_new)
                    if m == 0:
                        sk = sink_ref[h]
                        mx = jnp.maximum(mx, sk)
                    p = jnp.exp(s - mx)
                    p_new = jnp.exp(s_new - mx)
                    l = jnp.sum(p, axis=1, keepdims=True) + p_new
                    acc = jnp.sum(p * vt, axis=1, keepdims=True) + p_new * vn
                    if m == 0:
                        l = l + jnp.exp(sk - mx)
                        put(oat_ref, slot * HEAD_DIM, acc / l)
                    elif run[g] is None:
                        run[g] = (mx, l, acc)
                    else:
                        mx0, l0, acc0 = run[g]
                        top = jnp.maximum(mx0, mx)
                        a0 = jnp.exp(mx0 - top)
                        a1 = jnp.exp(mx - top)
                        run[g] = (top, a0 * l0 + a1 * l, a0 * acc0 + a1 * acc)
            for g in range(N_GROUP):
                _, l, acc = run[g]
                put(obt_ref, SLOT_OF_HEAD[N_GROUP * kvh + g] * HEAD_DIM, acc / l)

    for b in range(bb):
        per_b(b)


def _decode(layer, sink, zt, gq_col, gk_col, cos_col, sin_col, caches_t, prev_out, *, bb):
    nbatch = zt.shape[1]
    assert nbatch % bb == 0
    cache_specs = [pl.BlockSpec((None, bb) + c.shape[2:], lambda i: (layer, i, 0, 0, 0, 0)) for c in caches_t]
    n_alias = 0 if prev_out is None else len(prev_out)
    alias_specs = [pl.BlockSpec(memory_space=pl.ANY)] * n_alias
    n_in = 6 + len(caches_t)
    outs = pl.pallas_call(
        functools.partial(_decode_kernel, bb=bb, n_alias=n_alias),
        grid=(nbatch // bb,),
        in_specs=[pl.BlockSpec(memory_space=pltpu.SMEM), _const_spec(zt.shape),
                  _const_spec(gq_col.shape), _const_spec(gk_col.shape),
                  _const_spec(cos_col.shape), _const_spec(sin_col.shape)] + cache_specs + alias_specs,
        out_specs=[_const_spec((QW, nbatch)), _const_spec((QW, nbatch))] + cache_specs,
        out_shape=[jax.ShapeDtypeStruct((QW, nbatch), F32), jax.ShapeDtypeStruct((QW, nbatch), F32)]
                  + [jax.ShapeDtypeStruct(c.shape, F32) for c in caches_t],
        scratch_shapes=[pltpu.VMEM((N_MIX * QW, nbatch), F32), pltpu.VMEM((N_MIX * KVW, nbatch), F32)],
        input_output_aliases={n_in + j: 2 + j for j in range(n_alias)},
        compiler_params=_params(1),
        name="decode_attn_cache",
    )(sink, zt, gq_col, gk_col, cos_col, sin_col, *caches_t, *(prev_out or ()))
    return outs[0], outs[1], tuple(outs[2:])


def _merge_kernel(x_ref, h_ref, oa_ref, ob_ref, oc_ref, od_ref, wg_ref, bg_ref, wb_ref, wo_ref, out_ref):
    h = h_ref[...]
    slabs = lambda ref: jnp.concatenate([ref[c] for c in range(ref.shape[0])], axis=1)
    branches = (slabs(oa_ref), slabs(ob_ref), oc_ref[...], od_ref[...])
    merged = None
    for n, br in enumerate(branches):
        gate = jax.nn.sigmoid(_dot(h, wg_ref[n]) + bg_ref[n:n + 1])
        t = gate * _dot(br.astype(BF16), wb_ref[n])
        merged = t if merged is None else merged + t
    out_ref[...] = x_ref[...] + _dot(merged.astype(BF16), wo_ref[...])


def _merge(x2d, h, oa, ob, oc, od, wg, bg, wb, wo, *, tm):
    m_rows = x2d.shape[0]
    seq = oa.shape[2]
    assert m_rows % tm == 0 and seq % tm == 0
    tps = seq // tm
    row = lambda w: pl.BlockSpec((tm, w), lambda i: (i, 0))
    slab = pl.BlockSpec((None, QW // LANES, tm, LANES), lambda i: (i // tps, 0, i % tps, 0))
    return pl.pallas_call(
        _merge_kernel,
        grid=(m_rows // tm,),
        in_specs=[row(D_MODEL), row(D_MODEL), slab, slab, row(BRANCH_W), row(BRANCH_W)]
                 + [_const_spec(wg.shape), _const_spec(bg.shape), _const_spec(wb.shape), _const_spec(wo.shape)],
        out_specs=row(D_MODEL),
        out_shape=jax.ShapeDtypeStruct((m_rows, D_MODEL), F32),
        compiler_params=_params(1),
        name="branch_merge",
    )(x2d, h, oa, ob, oc, od, wg, bg, wb, wo)


def _ffn_kernel(x_ref, ln_ref, wi_ref, wo_ref, out_ref, act_ref):
    x = x_ref[...]
    h = _rms(x, ln_ref[...]).astype(BF16)
    for j in range(D_FF // FF_CHUNK):
        cols = slice(j * FF_CHUNK, (j + 1) * FF_CHUNK)
        gt = _dot(h, wi_ref[:, cols])
        up = _dot(h, wi_ref[:, D_FF + j * FF_CHUNK:D_FF + (j + 1) * FF_CHUNK])
        act_ref[:, cols] = (gt * jax.nn.sigmoid(gt) * up).astype(BF16)
    out_ref[...] = x + _dot(act_ref[...], wo_ref[...])


def _ffn(x2d, ln, wi, wo, *, tm):
    m_rows = x2d.shape[0]
    assert m_rows % tm == 0 and D_FF % FF_CHUNK == 0
    row = pl.BlockSpec((tm, D_MODEL), lambda i: (i, 0))
    return pl.pallas_call(
        _ffn_kernel,
        grid=(m_rows // tm,),
        in_specs=[row, _const_spec(ln.shape), _const_spec(wi.shape), _const_spec(wo.shape)],
        out_specs=row,
        out_shape=jax.ShapeDtypeStruct((m_rows, D_MODEL), F32),
        scratch_shapes=[pltpu.VMEM((tm, D_FF), BF16)],
        compiler_params=_params(1),
        name="swiglu_ffn",
    )(x2d, ln, wi, wo)


def _rope_angles(pos):
    inv_freq = ROPE_THETA ** (-2.0 * jnp.arange(ROT_HALF, dtype=F32) / ROT_DIM)
    ang = pos[:, None] * inv_freq[None, :]
    return jnp.cos(ang), jnp.sin(ang)


def _rope_row_tables(pos):
    cos, sin = _rope_angles(pos)
    n = pos.shape[0]
    zero = jnp.zeros((n, ROT_HALF), F32)
    rest0 = jnp.zeros((n, HEAD_DIM - ROT_DIM), F32)
    c = jnp.concatenate([cos, cos, jnp.ones((n, HEAD_DIM - ROT_DIM), F32)], axis=1)
    sa = jnp.concatenate([-sin, zero, rest0], axis=1)
    sb = jnp.concatenate([zero, sin, rest0], axis=1)
    return tuple(jnp.tile(t, (1, N_KV)) for t in (c, sa, sb))


def kernel(x_prompt, x_sample, cache_a, cache_b1, cache_b2, cache_b3, state_c, ln1, w_in, q_norm_a, k_norm_a,
           sink_a, q_norm_b, k_norm_b, conv_c, ws_d, bs_d, w_gate, b_gate, w_branch, w_o, ln2, w_ffn_in,
           w_ffn_out):
    bp, sp, _ = x_prompt.shape
    bs_n, ds, _ = x_sample.shape
    depth = ln1.shape[0]
    assert ds == 1 and sp >= CONV_W - 1
    caches = (cache_a, cache_b1, cache_b2, cache_b3)
    for c, (win, d) in zip(caches, MIXERS):
        assert c.shape[3] == win and win // d == BAND

    cos_p, sa_p, sb_p = _rope_row_tables(jnp.arange(sp, dtype=F32))
    cos_s, sin_s = _rope_angles(PAST_LEN + jnp.arange(ds, dtype=F32))
    cos_col, sin_col = cos_s.reshape(ROT_HALF, 1), sin_s.reshape(ROT_HALF, 1)
    caches_t = tuple(jnp.transpose(c, (0, 1, 2, 4, 5, 3)) for c in caches)

    y_p = x_prompt.reshape(bp * sp, D_MODEL)
    y_s = x_sample.reshape(bs_n, D_MODEL)
    kv_p, c_p, c_s, d_s = [], [], [], []
    new_t = None
    slot_cols = [h * HEAD_DIM + e for h in SLOT_HEADS for e in range(HEAD_DIM)]
    in_cols = jnp.asarray([m * ATT_W + c for m in range(N_MIX) for c in slot_cols + list(range(QW, ATT_W))]
                          + list(range(N_ATT, N_IN)), jnp.int32)
    slot_rows = jnp.asarray(slot_cols, jnp.int32)
    for l in range(depth):
        w_in_b = w_in[l][:, in_cols].astype(BF16)
        wb_l = jnp.concatenate([w_branch[l][:2][:, slot_rows], w_branch[l][2:]], axis=0)
        wg_b, wb_b, wo_b = w_gate[l].astype(BF16), wb_l.astype(BF16), w_o[l].astype(BF16)
        wfi_b, wfo_b = w_ffn_in[l].astype(BF16), w_ffn_out[l].astype(BF16)
        ln1_l, ln2_l = ln1[l][None], ln2[l][None]
        gq_all = jnp.concatenate([q_norm_a[l][None], q_norm_b[l]], axis=0)
        gk_all = jnp.concatenate([k_norm_a[l][None], k_norm_b[l]], axis=0)
        bst = jnp.repeat(bs_d[l].T, HEAD_DIM, axis=1)

        h, zatt, oc, od, newc = _inproj_prompt(y_p, ln1_l, w_in_b, conv_c[l], ws_d[l], bst, seq=sp, tm=ROW_TILE)
        oa, ob, kv = _attn_prompt(sink_a[l], zatt.reshape(N_ATT // LANES, bp, sp, LANES),
                                  jnp.tile(gq_all, (1, N_KV))[:, None], jnp.tile(gk_all, (1, N_KV))[:, None],
                                  cos_p, sa_p, sb_p)
        x1 = _merge(y_p, h, oa, ob, oc, od, wg_b, b_gate[l], wb_b, wo_b, tm=ROW_TILE)
        y_p = _ffn(x1, ln2_l, wfi_b, wfo_b, tm=ROW_TILE)
        kv_p.append(kv)
        c_p.append(newc)

        s0, s1 = state_c[l][:, 0], state_c[l][:, 1]
        w00 = jnp.repeat(ws_d[l][:, 0, 0], HEAD_DIM)[None]
        b0 = jnp.repeat(bs_d[l][:, 0], HEAD_DIM)[None]
        h, zatt, oc, od, zc, vd = _inproj_sample(y_s, ln1_l, w_in_b, conv_c[l], s0, s1, w00, b0)
        oat, obt, new_t = _decode(l, sink_a[l], zatt.T, gq_all.reshape(-1, 1), gk_all.reshape(-1, 1),
                                  cos_col, sin_col, caches_t, new_t, bb=DEC_BLOCK)
        to_slabs = lambda t: t.reshape(QW // LANES, LANES, bs_n).transpose(0, 2, 1)[None]
        x1 = _merge(y_s, h, to_slabs(oat), to_slabs(obt), oc, od, wg_b, b_gate[l], wb_b, wo_b, tm=bs_n)
        y_s = _ffn(x1, ln2_l, wfi_b, wfo_b, tm=bs_n)
        c_s.append(jnp.stack([s1, zc], axis=1))
        d_s.append(vd[:, None])

    def prompt_cache(m):
        win = min(MIXERS[m][0], sp)
        kvm = jnp.stack([kv[m] for kv in kv_p])[:, :, :, sp - win:]
        return kvm.reshape(depth, bp, 2, win, N_KV, HEAD_DIM)

    sample_cache = [jnp.transpose(t, (0, 1, 2, 5, 3, 4)) for t in new_t]
    return (y_p.reshape(bp, sp, D_MODEL), y_s.reshape(bs_n, ds, D_MODEL),
            prompt_cache(0), sample_cache[0], prompt_cache(1), sample_cache[1],
            prompt_cache(2), sample_cache[2], prompt_cache(3), sample_cache[3],
            jnp.stack(c_p), jnp.stack(c_s), jnp.stack(d_s))
```

```python
import functools
import math

import jax
import jax.numpy as jnp
from jax import lax
from jax.experimental import pallas as pl
from jax.experimental.pallas import tpu as pltpu

F32 = jnp.float32
BF16 = jnp.bfloat16

D_MODEL = 1024
HEAD_DIM = 64
ROT_DIM = HEAD_DIM // 4
ROT_HALF = ROT_DIM // 2
ROPE_THETA = 500000.0
BRANCH_W = 384
N_HEADS = BRANCH_W // HEAD_DIM
N_KV = 2
N_GROUP = N_HEADS // N_KV
QW = N_HEADS * HEAD_DIM
KVW = N_KV * HEAD_DIM
ATT_W = QW + 2 * KVW
WIN_A = 128
DIL_GROUPS = ((128, 1), (512, 4), (2048, 16))
MIXERS = ((WIN_A, 1),) + DIL_GROUPS
N_MIX = len(MIXERS)
BAND = 128
LANES = 128
SLOT_HEADS = (0, 3, 1, 4, 2, 5)
SLOT_OF_HEAD = tuple(SLOT_HEADS.index(h) for h in range(6))
CONV_W = 3
CHUNK = 128
N_SG = 6
N_BRANCH = 4
N_ATT = ATT_W * N_MIX
N_IN = N_ATT + 5 * BRANCH_W
D_FF = 2816
FF_CHUNK = 256
PAST_LEN = 16384
EPS = 1e-6
NEG = -1e30

VMEM_LIMIT_V7X = 56 * 1024 * 1024
ROW_TILE = 512
DEC_BLOCK = 2
BLOCK_UNROLL = 4
PROJ_CHUNK = 1280


def _params(n_axes, vmem=VMEM_LIMIT_V7X):
    return pltpu.CompilerParams(dimension_semantics=("arbitrary",) * n_axes, vmem_limit_bytes=vmem)


def _const_spec(shape):
    nd = len(shape)
    return pl.BlockSpec(shape, lambda *_: (0,) * nd)


def _rms(x, g):
    return x * lax.rsqrt(jnp.mean(x * x, axis=-1, keepdims=True) + EPS) * g


def _dot(a, b):
    return jnp.dot(a, b, preferred_element_type=F32)


def _project(x_ref, ln_ref, w_ref, h_ref, zatt_ref, slabs):
    h = _rms(x_ref[...], ln_ref[...]).astype(BF16)
    h_ref[...] = h
    pieces = []
    for lo in range(0, N_IN, PROJ_CHUNK):
        hi = min(lo + PROJ_CHUNK, N_IN)
        z = _dot(h, w_ref[:, lo:hi])
        pieces += [z[:, c * LANES:(c + 1) * LANES] for c in range((hi - lo) // LANES)]
    n_att = N_ATT // LANES
    if slabs:
        for c in range(n_att):
            zatt_ref[c] = pieces[c]
    else:
        zatt_ref[...] = jnp.concatenate(pieces[:n_att], axis=1)
    per = BRANCH_W // LANES
    return [jnp.concatenate(pieces[n_att + j * per:n_att + (j + 1) * per], axis=1) for j in range(5)]


def _inproj_prompt_kernel(x_ref, ln_ref, w_ref, conv_ref, ws_ref, bst_ref,
                          h_ref, zatt_ref, oc_ref, od_ref, newc_ref, carry_ref, *, tm, tiles_per_seq):
    i = pl.program_id(0)
    gate_b, gate_c, x_c, u_d, v_d = _project(x_ref, ln_ref, w_ref, h_ref, zatt_ref, True)

    @pl.when(i % tiles_per_seq == 0)
    def _():
        carry_ref[...] = jnp.zeros_like(carry_ref)

    zc = gate_c * x_c
    carry = carry_ref[...]
    row = lax.broadcasted_iota(jnp.int32, zc.shape, 0)
    z1 = jnp.where(row == 0, carry[1:2], pltpu.roll(zc, 1, 0))
    z2 = jnp.where(row == 0, carry[0:1], jnp.where(row == 1, carry[1:2], pltpu.roll(zc, 2, 0)))
    conv = conv_ref[0:1] * z2 + conv_ref[1:2] * z1 + conv_ref[2:3] * zc
    oc_ref[...] = gate_b * conv
    tail = zc[tm - 2:tm]
    carry_ref[...] = tail
    newc_ref[...] = tail

    tri = (lax.broadcasted_iota(jnp.int32, (CHUNK, CHUNK), 0)
           >= lax.broadcasted_iota(jnp.int32, (CHUNK, CHUNK), 1))
    vb = v_d.astype(BF16)
    nchunk = tm // CHUNK
    mixed = []
    for g in range(N_SG):
        wm = jnp.where(tri, ws_ref[g], 0.0).astype(BF16)
        vg = jnp.concatenate([vb[c * CHUNK:(c + 1) * CHUNK, g * HEAD_DIM:(g + 1) * HEAD_DIM]
                              for c in range(nchunk)], axis=1)
        mixed.append(_dot(wm, vg))
    for c in range(nchunk):
        rows = slice(c * CHUNK, (c + 1) * CHUNK)
        sv = jnp.concatenate([mixed[g][:, c * HEAD_DIM:(c + 1) * HEAD_DIM] for g in range(N_SG)], axis=1)
        od_ref[rows, :] = u_d[rows] * (sv + bst_ref[...])


def _inproj_sample_kernel(x_ref, ln_ref, w_ref, conv_ref, s0_ref, s1_ref, w00_ref, b0_ref,
                          h_ref, zatt_ref, oc_ref, od_ref, zc_ref, vd_ref):
    gate_b, gate_c, x_c, u_d, v_d = _project(x_ref, ln_ref, w_ref, h_ref, zatt_ref, False)
    zc = gate_c * x_c
    conv = conv_ref[0:1] * s0_ref[...] + conv_ref[1:2] * s1_ref[...] + conv_ref[2:3] * zc
    oc_ref[...] = gate_b * conv
    zc_ref[...] = zc
    od_ref[...] = u_d * (v_d * w00_ref[...] + b0_ref[...])
    vd_ref[...] = v_d


def _inproj_prompt(x2d, ln, w_bf, conv, ws, bst, *, seq, tm):
    m_rows = x2d.shape[0]
    assert m_rows % seq == 0 and seq % tm == 0 and tm % CHUNK == 0
    tps = seq // tm
    row = lambda w: pl.BlockSpec((tm, w), lambda i: (i, 0))
    return pl.pallas_call(
        functools.partial(_inproj_prompt_kernel, tm=tm, tiles_per_seq=tps),
        grid=(m_rows // tm,),
        in_specs=[row(D_MODEL), _const_spec((1, D_MODEL)), _const_spec((D_MODEL, N_IN)),
                  _const_spec((CONV_W, BRANCH_W)), _const_spec((N_SG, CHUNK, CHUNK)),
                  _const_spec((CHUNK, BRANCH_W))],
        out_specs=[row(D_MODEL), pl.BlockSpec((N_ATT // LANES, tm, LANES), lambda i: (0, i, 0)),
                   row(BRANCH_W), row(BRANCH_W),
                   pl.BlockSpec((None, CONV_W - 1, BRANCH_W), lambda i: (i // tps, 0, 0))],
        out_shape=[jax.ShapeDtypeStruct((m_rows, D_MODEL), BF16),
                   jax.ShapeDtypeStruct((N_ATT // LANES, m_rows, LANES), F32),
                   jax.ShapeDtypeStruct((m_rows, BRANCH_W), F32),
                   jax.ShapeDtypeStruct((m_rows, BRANCH_W), F32),
                   jax.ShapeDtypeStruct((m_rows // seq, CONV_W - 1, BRANCH_W), F32)],
        scratch_shapes=[pltpu.VMEM((CONV_W - 1, BRANCH_W), F32)],
        compiler_params=_params(1),
        name="inproj_prompt",
    )(x2d, ln, w_bf, conv, ws, bst)


def _inproj_sample(x2d, ln, w_bf, conv, s0, s1, w00, b0):
    n = x2d.shape[0]
    full = lambda w: _const_spec((n, w))
    return pl.pallas_call(
        _inproj_sample_kernel,
        grid=(1,),
        in_specs=[full(D_MODEL), _const_spec((1, D_MODEL)), _const_spec((D_MODEL, N_IN)),
                  _const_spec((CONV_W, BRANCH_W)), full(BRANCH_W), full(BRANCH_W),
                  _const_spec((1, BRANCH_W)), _const_spec((1, BRANCH_W))],
        out_specs=[full(D_MODEL), full(N_ATT)] + [full(BRANCH_W)] * 4,
        out_shape=[jax.ShapeDtypeStruct((n, D_MODEL), BF16), jax.ShapeDtypeStruct((n, N_ATT), F32)]
                  + [jax.ShapeDtypeStruct((n, BRANCH_W), F32)] * 4,
        compiler_params=_params(1),
        name="inproj_sample",
    )(x2d, ln, w_bf, conv, s0, s1, w00, b0)


def _segment_ones(n):
    r = lax.broadcasted_iota(jnp.int32, (n, n), 0) // HEAD_DIM
    c = lax.broadcasted_iota(jnp.int32, (n, n), 1) // HEAD_DIM
    return (r == c).astype(BF16)


def _head_norm_rope(xs, gains, ones, cos, sin_a, sin_b):
    n, rows = len(xs), xs[0].shape[0]
    sq = [x * x for x in xs]
    hi = [t.astype(BF16) for t in sq]
    lo = [(t - h.astype(F32)).astype(BF16) for t, h in zip(sq, hi)]
    ss = _dot(jnp.concatenate(hi + lo, axis=0), ones)
    out = []
    for c, (x, g) in enumerate(zip(xs, gains)):
        ssc = ss[c * rows:(c + 1) * rows] + ss[(n + c) * rows:(n + c + 1) * rows]
        xn = x * lax.rsqrt(ssc * (1.0 / HEAD_DIM) + EPS) * g
        out.append(xn * cos + pltpu.roll(xn, LANES - ROT_HALF, 1) * sin_a + pltpu.roll(xn, ROT_HALF, 1) * sin_b)
    return out


def _attn_mixer(d, idx, seq, sink_ref, z_ref, gq_ref, gk_ref, cos_ref, sa_ref, sb_ref,
                oa_ref, kv_ref, qs_ref, kt_ref, vs_ref, obs_ref, lse_ref, bias_ref):
    with_sink, first_group = idx == 0, idx == 1
    nblk = seq // (BAND * d)
    has_prev = nblk > 1
    shift = nblk.bit_length() - 1
    ones = _segment_ones(LANES)
    low = lax.broadcasted_iota(jnp.int32, (BAND, LANES), 1) < HEAD_DIM

    def natural_rows(it):
        if d == 1:
            return pl.ds(pl.multiple_of(it * BAND, BAND), BAND)
        return pl.ds((it >> shift) + (it & (nblk - 1)) * (BAND * d), BAND, stride=d)

    nb = seq // BAND
    nk = 2 * BAND if has_prev else BAND
    nq = QW // LANES
    ones_cols = jnp.ones((nk, LANES), BF16)

    def block_pos(b):
        return pl.ds(pl.multiple_of(b * BAND, BAND), BAND)

    def prev_block(b):
        return jnp.maximum(b - 1, 0)

    def prep_stage(it):
        rows = natural_rows(it)
        pos = block_pos(it)
        cos, sa, sb = cos_ref[rows, :], sa_ref[rows, :], sb_ref[rows, :]
        *qn, kn = _head_norm_rope([z_ref[c, rows, :] for c in range(nq + 1)],
                                  [gq_ref[...]] * nq + [gk_ref[...]], ones, cos, sa, sb)
        for c in range(nq):
            qc = (qn[c] * (1.0 / math.sqrt(HEAD_DIM))).astype(BF16)
            qs_ref[2 * c, pos, :] = jnp.where(low, qc, jnp.zeros_like(qc))
            qs_ref[2 * c + 1, pos, :] = jnp.where(low, jnp.zeros_like(qc), qc)
        v = z_ref[nq + 1, rows, :]
        kv_ref[0, rows, :] = kn
        kv_ref[1, rows, :] = v
        kt_ref[it] = kn.T.astype(BF16)
        vs_ref[pos, :] = v.astype(BF16)

    def score_stage(b):
        q6 = jnp.concatenate([qs_ref[j, block_pos(b), :] for j in range(N_HEADS)], axis=0)
        if has_prev:
            first = (b & (nblk - 1)) == 0
            kt = jnp.concatenate([kt_ref[prev_block(b)], kt_ref[b]], axis=1)
            bias = bias_ref[jnp.where(first, 1, 0)]
        else:
            kt = kt_ref[b]
            bias = bias_ref[1, :, BAND:]
        return _dot(q6, kt) + jnp.concatenate([bias] * N_HEADS, axis=0)

    def softmax_stage(s):
        mx = jnp.max(s, axis=1, keepdims=True)
        p = jnp.exp(s - mx).astype(BF16)
        return p, [jnp.where(low, mx[2 * c * BAND:(2 * c + 1) * BAND], mx[(2 * c + 1) * BAND:(2 * c + 2) * BAND])
                   for c in range(nq)]

    def value_stage(b, p):
        v = vs_ref[block_pos(b), :]
        if has_prev:
            v = jnp.concatenate([vs_ref[block_pos(prev_block(b)), :], v], axis=0)
        v_aug = jnp.concatenate([v, ones_cols], axis=1)
        return _dot(p, v_aug)

    def output_stage(b, r2, mxs):
        pos = block_pos(b)
        rows = natural_rows(b)
        for c in range(nq):
            ra, rb = slice(2 * c * BAND, (2 * c + 1) * BAND), slice((2 * c + 1) * BAND, (2 * c + 2) * BAND)
            dn = jnp.where(low, r2[ra, LANES:], r2[rb, LANES:])
            mx = mxs[c]
            if with_sink:
                dn = dn + jnp.exp(jnp.where(low, sink_ref[SLOT_HEADS[2 * c]], sink_ref[SLOT_HEADS[2 * c + 1]]) - mx)
            o = jnp.where(low, r2[ra, :LANES], r2[rb, :LANES]) / dn
            if with_sink:
                oa_ref[c, pos, :] = o
                continue
            lse = mx + jnp.log(dn)
            if first_group:
                obs_ref[c, rows, :] = o
                lse_ref[c, rows, :] = lse
            else:
                lse_old = lse_ref[c, rows, :]
                top = jnp.maximum(lse_old, lse)
                w_old = jnp.exp(lse_old - top)
                w_new = jnp.exp(lse - top)
                tot = w_old + w_new
                obs_ref[c, rows, :] = (w_old * obs_ref[c, rows, :] + w_new * o) / tot
                lse_ref[c, rows, :] = top + jnp.log(tot)

    def prep_step(b, carry):
        prep_stage(jnp.asarray(b, jnp.int32))
        return carry

    def attn_step(b, carry):
        b = jnp.asarray(b, jnp.int32)
        p, mxs = softmax_stage(score_stage(b))
        output_stage(b, value_stage(b, p), mxs)
        return carry

    lax.fori_loop(0, nb, prep_step, 0, unroll=BLOCK_UNROLL)
    lax.fori_loop(0, nb, attn_step, 0, unroll=BLOCK_UNROLL)


def _attn_prompt_kernel(sink_ref, z_ref, gq_ref, gk_ref, cos_ref, sa_ref, sb_ref,
                        oa_ref, ob_ref, kv_ref, qs_ref, kt_ref, vs_ref, obs_ref, lse_ref, bias_ref, *, seq):
    m = pl.program_id(1)
    qi = lax.broadcasted_iota(jnp.int32, (BAND, 2 * BAND), 0)
    kj = lax.broadcasted_iota(jnp.int32, (BAND, 2 * BAND), 1)
    in_band = kj <= qi + BAND
    bias_ref[0] = jnp.where(in_band & (kj >= qi), 0.0, NEG)
    bias_ref[1] = jnp.where(in_band & (kj >= BAND), 0.0, NEG)

    for idx, (_, d) in enumerate(MIXERS):
        @pl.when(m == idx)
        def _(idx=idx, d=d):
            _attn_mixer(d, idx, seq, sink_ref, z_ref, gq_ref, gk_ref, cos_ref, sa_ref, sb_ref,
                        oa_ref, kv_ref, qs_ref, kt_ref, vs_ref, obs_ref, lse_ref, bias_ref)

    @pl.when(m == N_MIX - 1)
    def _():
        ob_ref[...] = obs_ref[...]


def _attn_prompt(sink, z4, gq, gk, cos, sa, sb):
    _, nb, seq, _ = z4.shape
    assert all(seq % (BAND * d) == 0 for _, d in MIXERS)
    per = ATT_W // LANES
    nq = QW // LANES
    tab = _const_spec((seq, LANES))
    gain = pl.BlockSpec((None, 1, LANES), lambda b, m: (m, 0, 0))
    o_spec = pl.BlockSpec((None, nq, seq, LANES), lambda b, m: (b, 0, 0, 0))
    slabs = pltpu.VMEM((nq, seq, LANES), F32)
    return pl.pallas_call(
        functools.partial(_attn_prompt_kernel, seq=seq),
        grid=(nb, N_MIX),
        in_specs=[pl.BlockSpec(memory_space=pltpu.SMEM),
                  pl.BlockSpec((per, None, seq, LANES), lambda b, m: (m, b, 0, 0)),
                  gain, gain, tab, tab, tab],
        out_specs=[o_spec, o_spec,
                   pl.BlockSpec((None, None, 2, seq, LANES), lambda b, m: (m, b, 0, 0, 0))],
        out_shape=[jax.ShapeDtypeStruct((nb, nq, seq, LANES), F32),
                   jax.ShapeDtypeStruct((nb, nq, seq, LANES), F32),
                   jax.ShapeDtypeStruct((N_MIX, nb, 2, seq, LANES), F32)],
        scratch_shapes=[pltpu.VMEM((N_HEADS, seq, LANES), BF16),
                        pltpu.VMEM((seq // BAND, LANES, BAND), BF16),
                        pltpu.VMEM((seq, LANES), BF16),
                        slabs, slabs,
                        pltpu.VMEM((2, BAND, 2 * BAND), F32)],
        compiler_params=_params(2),
        name="attn_prompt",
    )(sink, z4, gq, gk, cos, sa, sb)


def _column_norm_rope(x, g, cos, sin):
    ss = jnp.sum(x * x, axis=0, keepdims=True)
    xn = x * lax.rsqrt(ss * (1.0 / HEAD_DIM) + EPS) * g
    x1, x2 = xn[0:ROT_HALF], xn[ROT_HALF:ROT_DIM]
    return jnp.concatenate([x1 * cos - x2 * sin, x2 * cos + x1 * sin, xn[ROT_DIM:]], axis=0)


def _decode_kernel(sink_ref, zt_ref, gq_ref, gk_ref, cos_ref, sin_ref,
                   ca_ref, c1_ref, c2_ref, c3_ref, *rest, bb, n_alias):
    rest = rest[n_alias:]
    oat_ref, obt_ref, na_ref, n1_ref, n2_ref, n3_ref, qn_ref, kn_ref, vn_ref = rest
    i = pl.program_id(0)
    nbatch = zt_ref.shape[1]
    cache_refs = (ca_ref, c1_ref, c2_ref, c3_ref)
    new_refs = (na_ref, n1_ref, n2_ref, n3_ref)

    @pl.when(i == 0)
    def _():
        cos, sin = cos_ref[...], sin_ref[...]
        for m in range(N_MIX):
            for h in range(N_HEADS):
                r = m * ATT_W + h * HEAD_DIM
                o = m * QW + h * HEAD_DIM
                qn_ref[o:o + HEAD_DIM, :] = _column_norm_rope(
                    zt_ref[r:r + HEAD_DIM, :], gq_ref[m * HEAD_DIM:(m + 1) * HEAD_DIM, :], cos, sin
                ) * (1.0 / math.sqrt(HEAD_DIM))
            for kvh in range(N_KV):
                r = m * ATT_W + QW + kvh * HEAD_DIM
                o = m * KVW + kvh * HEAD_DIM
                kn_ref[o:o + HEAD_DIM, :] = _column_norm_rope(
                    zt_ref[r:r + HEAD_DIM, :], gk_ref[m * HEAD_DIM:(m + 1) * HEAD_DIM, :], cos, sin)
            vn_ref[m * KVW:(m + 1) * KVW, :] = zt_ref[m * ATT_W + QW + KVW:(m + 1) * ATT_W, :]
        oat_ref[...] = jnp.zeros_like(oat_ref)
        obt_ref[...] = jnp.zeros_like(obt_ref)

    lane_b = lax.broadcasted_iota(jnp.int32, (1, nbatch), 1)

    def per_head(x):
        return jnp.concatenate([jnp.broadcast_to(x[h:h + 1], (HEAD_DIM, 1)) for h in range(N_HEADS)], axis=0)

    def slot_rows(col):
        return jnp.concatenate([col[h * HEAD_DIM:(h + 1) * HEAD_DIM] for h in SLOT_HEADS], axis=0)

    def per_b(b):
        sel = lane_b == (i * bb + b)

        def column(ref):
            return jnp.sum(jnp.where(sel, ref[...], 0.0), axis=1, keepdims=True)

        qcol, kcol, vcol = column(qn_ref), column(kn_ref), column(vn_ref)
        run = None
        for m, (win, d) in enumerate(MIXERS):
            cref, nref = cache_refs[m], new_refs[m]
            last = lax.broadcasted_iota(jnp.int32, (HEAD_DIM, win), 1) == win - 1
            s_rows, new_rows, kv_new, vts = [], [], [], []
            for kvh in range(N_KV):
                kt, vt = cref[b, 0, kvh], cref[b, 1, kvh]
                r = m * KVW + kvh * HEAD_DIM
                kn, vn = kcol[r:r + HEAD_DIM], vcol[r:r + HEAD_DIM]
                nref[b, 0, kvh] = jnp.where(last, kn, pltpu.roll(kt, win - 1, 1))
                nref[b, 1, kvh] = jnp.where(last, vn, pltpu.roll(vt, win - 1, 1))
                for g in range(N_GROUP):
                    r = m * QW + SLOT_OF_HEAD[N_GROUP * kvh + g] * HEAD_DIM
                    qc = qcol[r:r + HEAD_DIM]
                    s_rows.append(jnp.sum(qc * kt, axis=0, keepdims=True))
                    new_rows.append(jnp.sum(qc * kn, axis=0, keepdims=True))
                vts.append(vt)
                kv_new.append(vn)
            s, s_new = jnp.concatenate(s_rows, axis=0), jnp.concatenate(new_rows, axis=0)
            if d > 1:
                s = jnp.where((lax.broadcasted_iota(jnp.int32, (1, win), 1) % d) == 0, s, NEG)
            mx = jnp.maximum(jnp.max(s, axis=1, keepdims=True), s_new)
            if m == 0:
                sink = jnp.concatenate([jnp.full((1, 1), sink_ref[h], F32) for h in range(N_HEADS)], axis=0)
                mx = jnp.maximum(mx, sink)
            p, p_new = jnp.exp(s - mx), jnp.exp(s_new - mx)
            l = jnp.sum(p, axis=1, keepdims=True) + p_new
            if m == 0:
                l = l + jnp.exp(sink - mx)
            folded = []
            for h in range(N_HEADS):
                pv = p[h:h + 1] * vts[h // N_GROUP]
                f = pv[:, 0:LANES]
                for j in range(1, win // LANES):
                    f = f + pv[:, j * LANES:(j + 1) * LANES]
                folded.append(f)
            acc = jnp.sum(jnp.concatenate(folded, axis=0), axis=1, keepdims=True)
            acc = acc + per_head(p_new) * jnp.concatenate([kv_new[h // N_GROUP] for h in range(N_HEADS)], axis=0)
            if m == 0:
                oat_ref[...] = jnp.where(sel, slot_rows(acc / per_head(l)), oat_ref[...])
            elif run is None:
                run = (mx, l, acc)
            else:
                mx0, l0, acc0 = run
                top = jnp.maximum(mx0, mx)
                a0, a1 = jnp.exp(mx0 - top), jnp.exp(mx - top)
                run = (top, a0 * l0 + a1 * l, per_head(a0) * acc0 + per_head(a1) * acc)
        _, l, acc = run
        obt_ref[...] = jnp.where(sel, slot_rows(acc / per_head(l)), obt_ref[...])

    for b in range(bb):
        per_b(b)


def _decode(layer, sink, zt, gq_col, gk_col, cos_col, sin_col, caches_t, prev_out, *, bb):
    nbatch = zt.shape[1]
    assert nbatch % bb == 0
    cache_specs = [pl.BlockSpec((None, bb) + c.shape[2:], lambda i: (layer, i, 0, 0, 0, 0)) for c in caches_t]
    n_alias = 0 if prev_out is None else len(prev_out)
    alias_specs = [pl.BlockSpec(memory_space=pl.ANY)] * n_alias
    n_in = 6 + len(caches_t)
    outs = pl.pallas_call(
        functools.partial(_decode_kernel, bb=bb, n_alias=n_alias),
        grid=(nbatch // bb,),
        in_specs=[pl.BlockSpec(memory_space=pltpu.SMEM), _const_spec(zt.shape),
                  _const_spec(gq_col.shape), _const_spec(gk_col.shape),
                  _const_spec(cos_col.shape), _const_spec(sin_col.shape)] + cache_specs + alias_specs,
        out_specs=[_const_spec((QW, nbatch)), _const_spec((QW, nbatch))] + cache_specs,
        out_shape=[jax.ShapeDtypeStruct((QW, nbatch), F32), jax.ShapeDtypeStruct((QW, nbatch), F32)]
                  + [jax.ShapeDtypeStruct(c.shape, F32) for c in caches_t],
        scratch_shapes=[pltpu.VMEM((N_MIX * QW, nbatch), F32), pltpu.VMEM((N_MIX * KVW, nbatch), F32),
                        pltpu.VMEM((N_MIX * KVW, nbatch), F32)],
        input_output_aliases={n_in + j: 2 + j for j in range(n_alias)},
        compiler_params=_params(1),
        name="decode_attn_cache",
    )(sink, zt, gq_col, gk_col, cos_col, sin_col, *caches_t, *(prev_out or ()))
    return outs[0], outs[1], tuple(outs[2:])


def _merge_kernel(x_ref, h_ref, oa_ref, ob_ref, oc_ref, od_ref, wg_ref, bg_ref, wb_ref, wo_ref, out_ref):
    h = h_ref[...]
    slabs = lambda ref: jnp.concatenate([ref[c] for c in range(ref.shape[0])], axis=1)
    branches = (slabs(oa_ref), slabs(ob_ref), oc_ref[...], od_ref[...])
    merged = None
    for n, br in enumerate(branches):
        gate = jax.nn.sigmoid(_dot(h, wg_ref[n]) + bg_ref[n:n + 1])
        t = gate * _dot(br.astype(BF16), wb_ref[n])
        merged = t if merged is None else merged + t
    out_ref[...] = x_ref[...] + _dot(merged.astype(BF16), wo_ref[...])


def _merge(x2d, h, oa, ob, oc, od, wg, bg, wb, wo, *, tm):
    m_rows = x2d.shape[0]
    seq = oa.shape[2]
    assert m_rows % tm == 0 and seq % tm == 0
    tps = seq // tm
    row = lambda w: pl.BlockSpec((tm, w), lambda i: (i, 0))
    slab = pl.BlockSpec((None, QW // LANES, tm, LANES), lambda i: (i // tps, 0, i % tps, 0))
    return pl.pallas_call(
        _merge_kernel,
        grid=(m_rows // tm,),
        in_specs=[row(D_MODEL), row(D_MODEL), slab, slab, row(BRANCH_W), row(BRANCH_W)]
                 + [_const_spec(wg.shape), _const_spec(bg.shape), _const_spec(wb.shape), _const_spec(wo.shape)],
        out_specs=row(D_MODEL),
        out_shape=jax.ShapeDtypeStruct((m_rows, D_MODEL), F32),
        compiler_params=_params(1),
        name="branch_merge",
    )(x2d, h, oa, ob, oc, od, wg, bg, wb, wo)


def _ffn_kernel(x_ref, ln_ref, wi_ref, wo_ref, out_ref, act_ref):
    x = x_ref[...]
    h = _rms(x, ln_ref[...]).astype(BF16)
    for j in range(D_FF // FF_CHUNK):
        cols = slice(j * FF_CHUNK, (j + 1) * FF_CHUNK)
        gt = _dot(h, wi_ref[:, cols])
        up = _dot(h, wi_ref[:, D_FF + j * FF_CHUNK:D_FF + (j + 1) * FF_CHUNK])
        act_ref[:, cols] = (gt * jax.nn.sigmoid(gt) * up).astype(BF16)
    out_ref[...] = x + _dot(act_ref[...], wo_ref[...])


def _ffn(x2d, ln, wi, wo, *, tm):
    m_rows = x2d.shape[0]
    assert m_rows % tm == 0 and D_FF % FF_CHUNK == 0
    row = pl.BlockSpec((tm, D_MODEL), lambda i: (i, 0))
    return pl.pallas_call(
        _ffn_kernel,
        grid=(m_rows // tm,),
        in_specs=[row, _const_spec(ln.shape), _const_spec(wi.shape), _const_spec(wo.shape)],
        out_specs=row,
        out_shape=jax.ShapeDtypeStruct((m_rows, D_MODEL), F32),
        scratch_shapes=[pltpu.VMEM((tm, D_FF), BF16)],
        compiler_params=_params(1),
        name="swiglu_ffn",
    )(x2d, ln, wi, wo)


def _rope_angles(pos):
    inv_freq = ROPE_THETA ** (-2.0 * jnp.arange(ROT_HALF, dtype=F32) / ROT_DIM)
    ang = pos[:, None] * inv_freq[None, :]
    return jnp.cos(ang), jnp.sin(ang)


def _rope_row_tables(pos):
    cos, sin = _rope_angles(pos)
    n = pos.shape[0]
    zero = jnp.zeros((n, ROT_HALF), F32)
    rest0 = jnp.zeros((n, HEAD_DIM - ROT_DIM), F32)
    c = jnp.concatenate([cos, cos, jnp.ones((n, HEAD_DIM - ROT_DIM), F32)], axis=1)
    sa = jnp.concatenate([-sin, zero, rest0], axis=1)
    sb = jnp.concatenate([zero, sin, rest0], axis=1)
    return tuple(jnp.tile(t, (1, N_KV)) for t in (c, sa, sb))


def kernel(x_prompt, x_sample, cache_a, cache_b1, cache_b2, cache_b3, state_c, ln1, w_in, q_norm_a, k_norm_a,
           sink_a, q_norm_b, k_norm_b, conv_c, ws_d, bs_d, w_gate, b_gate, w_branch, w_o, ln2, w_ffn_in,
           w_ffn_out):
    bp, sp, _ = x_prompt.shape
    bs_n, ds, _ = x_sample.shape
    depth = ln1.shape[0]
    assert ds == 1 and sp >= CONV_W - 1
    caches = (cache_a, cache_b1, cache_b2, cache_b3)
    for c, (win, d) in zip(caches, MIXERS):
        assert c.shape[3] == win and win // d == BAND

    cos_p, sa_p, sb_p = _rope_row_tables(jnp.arange(sp, dtype=F32))
    cos_s, sin_s = _rope_angles(PAST_LEN + jnp.arange(ds, dtype=F32))
    cos_col, sin_col = cos_s.reshape(ROT_HALF, 1), sin_s.reshape(ROT_HALF, 1)
    caches_t = tuple(jnp.transpose(c, (0, 1, 2, 4, 5, 3)) for c in caches)

    y_p = x_prompt.reshape(bp * sp, D_MODEL)
    y_s = x_sample.reshape(bs_n, D_MODEL)
    kv_p, c_p, c_s, d_s = [], [], [], []
    new_t = None
    slot_cols = [h * HEAD_DIM + e for h in SLOT_HEADS for e in range(HEAD_DIM)]
    in_cols = jnp.asarray([m * ATT_W + c for m in range(N_MIX) for c in slot_cols + list(range(QW, ATT_W))]
                          + list(range(N_ATT, N_IN)), jnp.int32)
    slot_rows = jnp.asarray(slot_cols, jnp.int32)
    for l in range(depth):
        w_in_b = w_in[l][:, in_cols].astype(BF16)
        wb_l = jnp.concatenate([w_branch[l][:2][:, slot_rows], w_branch[l][2:]], axis=0)
        wg_b, wb_b, wo_b = w_gate[l].astype(BF16), wb_l.astype(BF16), w_o[l].astype(BF16)
        wfi_b, wfo_b = w_ffn_in[l].astype(BF16), w_ffn_out[l].astype(BF16)
        ln1_l, ln2_l = ln1[l][None], ln2[l][None]
        gq_all = jnp.concatenate([q_norm_a[l][None], q_norm_b[l]], axis=0)
        gk_all = jnp.concatenate([k_norm_a[l][None], k_norm_b[l]], axis=0)
        bst = jnp.repeat(bs_d[l].T, HEAD_DIM, axis=1)

        h, zatt, oc, od, newc = _inproj_prompt(y_p, ln1_l, w_in_b, conv_c[l], ws_d[l], bst, seq=sp, tm=ROW_TILE)
        oa, ob, kv = _attn_prompt(sink_a[l], zatt.reshape(N_ATT // LANES, bp, sp, LANES),
                                  jnp.tile(gq_all, (1, N_KV))[:, None], jnp.tile(gk_all, (1, N_KV))[:, None],
                                  cos_p, sa_p, sb_p)
        x1 = _merge(y_p, h, oa, ob, oc, od, wg_b, b_gate[l], wb_b, wo_b, tm=ROW_TILE)
        y_p = _ffn(x1, ln2_l, wfi_b, wfo_b, tm=ROW_TILE)
        kv_p.append(kv)
        c_p.append(newc)

        s0, s1 = state_c[l][:, 0], state_c[l][:, 1]
        w00 = jnp.repeat(ws_d[l][:, 0, 0], HEAD_DIM)[None]
        b0 = jnp.repeat(bs_d[l][:, 0], HEAD_DIM)[None]
        h, zatt, oc, od, zc, vd = _inproj_sample(y_s, ln1_l, w_in_b, conv_c[l], s0, s1, w00, b0)
        oat, obt, new_t = _decode(l, sink_a[l], zatt.T, gq_all.reshape(-1, 1), gk_all.reshape(-1, 1),
                                  cos_col, sin_col, caches_t, new_t, bb=DEC_BLOCK)
        to_slabs = lambda t: t.reshape(QW // LANES, LANES, bs_n).transpose(0, 2, 1)[None]
        x1 = _merge(y_s, h, to_slabs(oat), to_slabs(obt), oc, od, wg_b, b_gate[l], wb_b, wo_b, tm=bs_n)
        y_s = _ffn(x1, ln2_l, wfi_b, wfo_b, tm=bs_n)
        c_s.append(jnp.stack([s1, zc], axis=1))
        d_s.append(vd[:, None])

    def prompt_cache(m):
        win = min(MIXERS[m][0], sp)
        kvm = jnp.stack([kv[m] for kv in kv_p])[:, :, :, sp - win:]
        return kvm.reshape(depth, bp, 2, win, N_KV, HEAD_DIM)

    sample_cache = [jnp.transpose(t, (0, 1, 2, 5, 3, 4)) for t in new_t]
    return (y_p.reshape(bp, sp, D_MODEL), y_s.reshape(bs_n, ds, D_MODEL),
            prompt_cache(0), sample_cache[0], prompt_cache(1), sample_cache[1],
            prompt_cache(2), sample_cache[2], prompt_cache(3), sample_cache[3],
            jnp.stack(c_p), jnp.stack(c_s), jnp.stack(d_s))
```

```python
import functools
import math

import jax
import jax.numpy as jnp
from jax import lax
from jax.experimental import pallas as pl
from jax.experimental.pallas import tpu as pltpu

F32 = jnp.float32
BF16 = jnp.bfloat16

D_MODEL = 1024
HEAD_DIM = 64
ROT_DIM = HEAD_DIM // 4
ROT_HALF = ROT_DIM // 2
ROPE_THETA = 500000.0
BRANCH_W = 384
N_HEADS = BRANCH_W // HEAD_DIM
N_KV = 2
N_GROUP = N_HEADS // N_KV
QW = N_HEADS * HEAD_DIM
KVW = N_KV * HEAD_DIM
ATT_W = QW + 2 * KVW
WIN_A = 128
DIL_GROUPS = ((128, 1), (512, 4), (2048, 16))
MIXERS = ((WIN_A, 1),) + DIL_GROUPS
N_MIX = len(MIXERS)
BAND = 128
LANES = 128
SLOT_HEADS = (0, 3, 1, 4, 2, 5)
SLOT_OF_HEAD = tuple(SLOT_HEADS.index(h) for h in range(6))
CONV_W = 3
CHUNK = 128
N_SG = 6
N_BRANCH = 4
N_ATT = ATT_W * N_MIX
N_IN = N_ATT + 5 * BRANCH_W
D_FF = 2816
FF_CHUNK = 256
PAST_LEN = 16384
EPS = 1e-6
NEG = -1e30

VMEM_LIMIT_V7X = 56 * 1024 * 1024
ROW_TILE = 512
DEC_BLOCK = 2
BLOCK_UNROLL = 4
PROJ_CHUNK = 1280


def _params(n_axes, vmem=VMEM_LIMIT_V7X):
    return pltpu.CompilerParams(dimension_semantics=("arbitrary",) * n_axes, vmem_limit_bytes=vmem)


def _const_spec(shape):
    nd = len(shape)
    return pl.BlockSpec(shape, lambda *_: (0,) * nd)


def _layer_spec(stacked, layer):
    shape = stacked.shape[1:]
    return pl.BlockSpec((None,) + shape, lambda *_: (layer,) + (0,) * len(shape))


def _rms(x, g):
    return x * lax.rsqrt(jnp.mean(x * x, axis=-1, keepdims=True) + EPS) * g


def _dot(a, b):
    return jnp.dot(a, b, preferred_element_type=F32)


def _project(x_ref, ln_ref, w_ref, h_ref, zatt_ref, slabs):
    h = _rms(x_ref[...], ln_ref[...]).astype(BF16)
    h_ref[...] = h
    pieces = []
    for lo in range(0, N_IN, PROJ_CHUNK):
        hi = min(lo + PROJ_CHUNK, N_IN)
        z = _dot(h, w_ref[:, lo:hi])
        pieces += [z[:, c * LANES:(c + 1) * LANES] for c in range((hi - lo) // LANES)]
    n_att = N_ATT // LANES
    if slabs:
        for c in range(n_att):
            zatt_ref[c] = pieces[c]
    else:
        zatt_ref[...] = jnp.concatenate(pieces[:n_att], axis=1)
    per = BRANCH_W // LANES
    return [jnp.concatenate(pieces[n_att + j * per:n_att + (j + 1) * per], axis=1) for j in range(5)]


def _inproj_prompt_kernel(x_ref, ln_ref, w_ref, conv_ref, ws_ref, bst_ref,
                          h_ref, zatt_ref, oc_ref, od_ref, newc_ref, carry_ref, *, tm, tiles_per_seq):
    i = pl.program_id(0)
    gate_b, gate_c, x_c, u_d, v_d = _project(x_ref, ln_ref, w_ref, h_ref, zatt_ref, True)

    @pl.when(i % tiles_per_seq == 0)
    def _():
        carry_ref[...] = jnp.zeros_like(carry_ref)

    zc = gate_c * x_c
    carry = carry_ref[...]
    row = lax.broadcasted_iota(jnp.int32, zc.shape, 0)
    z1 = jnp.where(row == 0, carry[1:2], pltpu.roll(zc, 1, 0))
    z2 = jnp.where(row == 0, carry[0:1], jnp.where(row == 1, carry[1:2], pltpu.roll(zc, 2, 0)))
    conv = conv_ref[0:1] * z2 + conv_ref[1:2] * z1 + conv_ref[2:3] * zc
    oc_ref[...] = gate_b * conv
    tail = zc[tm - 2:tm]
    carry_ref[...] = tail
    newc_ref[...] = tail

    tri = (lax.broadcasted_iota(jnp.int32, (CHUNK, CHUNK), 0)
           >= lax.broadcasted_iota(jnp.int32, (CHUNK, CHUNK), 1))
    vb = v_d.astype(BF16)
    nchunk = tm // CHUNK
    mixed = []
    for g in range(N_SG):
        wm = jnp.where(tri, ws_ref[g], 0.0).astype(BF16)
        vg = jnp.concatenate([vb[c * CHUNK:(c + 1) * CHUNK, g * HEAD_DIM:(g + 1) * HEAD_DIM]
                              for c in range(nchunk)], axis=1)
        mixed.append(_dot(wm, vg))
    for c in range(nchunk):
        rows = slice(c * CHUNK, (c + 1) * CHUNK)
        sv = jnp.concatenate([mixed[g][:, c * HEAD_DIM:(c + 1) * HEAD_DIM] for g in range(N_SG)], axis=1)
        od_ref[rows, :] = u_d[rows] * (sv + bst_ref[...])


def _inproj_sample_kernel(x_ref, ln_ref, w_ref, conv_ref, s0_ref, s1_ref, w00_ref, b0_ref,
                          h_ref, zatt_ref, oc_ref, od_ref, zc_ref, vd_ref):
    gate_b, gate_c, x_c, u_d, v_d = _project(x_ref, ln_ref, w_ref, h_ref, zatt_ref, False)
    zc = gate_c * x_c
    conv = conv_ref[0:1] * s0_ref[...] + conv_ref[1:2] * s1_ref[...] + conv_ref[2:3] * zc
    oc_ref[...] = gate_b * conv
    zc_ref[...] = zc
    od_ref[...] = u_d * (v_d * w00_ref[...] + b0_ref[...])
    vd_ref[...] = v_d


def _inproj_prompt(layer, x2d, ln, w_bf, conv, ws, bst, *, seq, tm):
    m_rows = x2d.shape[0]
    assert m_rows % seq == 0 and seq % tm == 0 and tm % CHUNK == 0
    tps = seq // tm
    row = lambda w: pl.BlockSpec((tm, w), lambda i: (i, 0))
    return pl.pallas_call(
        functools.partial(_inproj_prompt_kernel, tm=tm, tiles_per_seq=tps),
        grid=(m_rows // tm,),
        in_specs=[row(D_MODEL), _const_spec((1, D_MODEL)), _layer_spec(w_bf, layer),
                  _const_spec((CONV_W, BRANCH_W)), _const_spec((N_SG, CHUNK, CHUNK)),
                  _const_spec((CHUNK, BRANCH_W))],
        out_specs=[row(D_MODEL), pl.BlockSpec((N_ATT // LANES, tm, LANES), lambda i: (0, i, 0)),
                   row(BRANCH_W), row(BRANCH_W),
                   pl.BlockSpec((None, CONV_W - 1, BRANCH_W), lambda i: (i // tps, 0, 0))],
        out_shape=[jax.ShapeDtypeStruct((m_rows, D_MODEL), BF16),
                   jax.ShapeDtypeStruct((N_ATT // LANES, m_rows, LANES), F32),
                   jax.ShapeDtypeStruct((m_rows, BRANCH_W), F32),
                   jax.ShapeDtypeStruct((m_rows, BRANCH_W), F32),
                   jax.ShapeDtypeStruct((m_rows // seq, CONV_W - 1, BRANCH_W), F32)],
        scratch_shapes=[pltpu.VMEM((CONV_W - 1, BRANCH_W), F32)],
        compiler_params=_params(1),
        name="inproj_prompt",
    )(x2d, ln, w_bf, conv, ws, bst)


def _inproj_sample(layer, x2d, ln, w_bf, conv, s0, s1, w00, b0):
    n = x2d.shape[0]
    full = lambda w: _const_spec((n, w))
    return pl.pallas_call(
        _inproj_sample_kernel,
        grid=(1,),
        in_specs=[full(D_MODEL), _const_spec((1, D_MODEL)), _layer_spec(w_bf, layer),
                  _const_spec((CONV_W, BRANCH_W)), full(BRANCH_W), full(BRANCH_W),
                  _const_spec((1, BRANCH_W)), _const_spec((1, BRANCH_W))],
        out_specs=[full(D_MODEL), full(N_ATT)] + [full(BRANCH_W)] * 4,
        out_shape=[jax.ShapeDtypeStruct((n, D_MODEL), BF16), jax.ShapeDtypeStruct((n, N_ATT), F32)]
                  + [jax.ShapeDtypeStruct((n, BRANCH_W), F32)] * 4,
        compiler_params=_params(1),
        name="inproj_sample",
    )(x2d, ln, w_bf, conv, s0, s1, w00, b0)


def _segment_ones(n):
    r = lax.broadcasted_iota(jnp.int32, (n, n), 0) // HEAD_DIM
    c = lax.broadcasted_iota(jnp.int32, (n, n), 1) // HEAD_DIM
    return (r == c).astype(BF16)


def _head_norm_rope(xs, gains, ones, cos, sin_a, sin_b):
    n, rows = len(xs), xs[0].shape[0]
    sq = [x * x for x in xs]
    hi = [t.astype(BF16) for t in sq]
    lo = [(t - h.astype(F32)).astype(BF16) for t, h in zip(sq, hi)]
    ss = _dot(jnp.concatenate(hi + lo, axis=0), ones)
    out = []
    for c, (x, g) in enumerate(zip(xs, gains)):
        ssc = ss[c * rows:(c + 1) * rows] + ss[(n + c) * rows:(n + c + 1) * rows]
        xn = x * lax.rsqrt(ssc * (1.0 / HEAD_DIM) + EPS) * g
        out.append(xn * cos + pltpu.roll(xn, LANES - ROT_HALF, 1) * sin_a + pltpu.roll(xn, ROT_HALF, 1) * sin_b)
    return out


def _attn_mixer(d, idx, seq, sink_ref, z_ref, gq_ref, gk_ref, cos_ref, sa_ref, sb_ref,
                oa_ref, cache_ref, qs_ref, kt_ref, vs_ref, kn_ref, obs_ref, lse_ref, bias_ref):
    with_sink, first_group = idx == 0, idx == 1
    nblk = seq // (BAND * d)
    has_prev = nblk > 1
    shift = nblk.bit_length() - 1
    ones = _segment_ones(LANES)
    low = lax.broadcasted_iota(jnp.int32, (BAND, LANES), 1) < HEAD_DIM

    def natural_rows(it):
        if d == 1:
            return pl.ds(pl.multiple_of(it * BAND, BAND), BAND)
        return pl.ds((it >> shift) + (it & (nblk - 1)) * (BAND * d), BAND, stride=d)

    nb = seq // BAND
    nk = 2 * BAND if has_prev else BAND
    nq = QW // LANES
    ones_cols = jnp.ones((nk, LANES), BF16)

    def block_pos(b):
        return pl.ds(pl.multiple_of(b * BAND, BAND), BAND)

    def prev_block(b):
        return jnp.maximum(b - 1, 0)

    def prep_stage(it):
        rows = natural_rows(it)
        pos = block_pos(it)
        cos, sa, sb = cos_ref[rows, :], sa_ref[rows, :], sb_ref[rows, :]
        *qn, kn = _head_norm_rope([z_ref[c, rows, :] for c in range(nq + 1)],
                                  [gq_ref[...]] * nq + [gk_ref[...]], ones, cos, sa, sb)
        for c in range(nq):
            qc = (qn[c] * (1.0 / math.sqrt(HEAD_DIM))).astype(BF16)
            qs_ref[2 * c, pos, :] = jnp.where(low, qc, jnp.zeros_like(qc))
            qs_ref[2 * c + 1, pos, :] = jnp.where(low, jnp.zeros_like(qc), qc)
        kn_ref[rows, :] = kn
        kt_ref[it] = kn.T.astype(BF16)
        vs_ref[pos, :] = z_ref[nq + 1, rows, :].astype(BF16)

    def cache_stage():
        win = cache_ref.shape[2]
        for j in range(win // BAND):
            t0 = seq - win + j * BAND
            cache_ref[0, :, j * BAND:(j + 1) * BAND] = kn_ref[t0:t0 + BAND, :].T
            cache_ref[1, :, j * BAND:(j + 1) * BAND] = z_ref[nq + 1, t0:t0 + BAND, :].T

    def score_stage(b):
        q6 = jnp.concatenate([qs_ref[j, block_pos(b), :] for j in range(N_HEADS)], axis=0)
        if has_prev:
            first = (b & (nblk - 1)) == 0
            kt = jnp.concatenate([kt_ref[prev_block(b)], kt_ref[b]], axis=1)
            bias = bias_ref[jnp.where(first, 1, 0)]
        else:
            kt = kt_ref[b]
            bias = bias_ref[1, :, BAND:]
        return _dot(q6, kt) + jnp.concatenate([bias] * N_HEADS, axis=0)

    def softmax_stage(s):
        mx = jnp.max(s, axis=1, keepdims=True)
        p = jnp.exp(s - mx).astype(BF16)
        return p, [jnp.where(low, mx[2 * c * BAND:(2 * c + 1) * BAND], mx[(2 * c + 1) * BAND:(2 * c + 2) * BAND])
                   for c in range(nq)]

    def value_stage(b, p):
        v = vs_ref[block_pos(b), :]
        if has_prev:
            v = jnp.concatenate([vs_ref[block_pos(prev_block(b)), :], v], axis=0)
        v_aug = jnp.concatenate([v, ones_cols], axis=1)
        return _dot(p, v_aug)

    def output_stage(b, r2, mxs):
        pos = block_pos(b)
        rows = natural_rows(b)
        for c in range(nq):
            ra, rb = slice(2 * c * BAND, (2 * c + 1) * BAND), slice((2 * c + 1) * BAND, (2 * c + 2) * BAND)
            dn = jnp.where(low, r2[ra, LANES:], r2[rb, LANES:])
            mx = mxs[c]
            if with_sink:
                dn = dn + jnp.exp(jnp.where(low, sink_ref[SLOT_HEADS[2 * c]], sink_ref[SLOT_HEADS[2 * c + 1]]) - mx)
            o = jnp.where(low, r2[ra, :LANES], r2[rb, :LANES]) / dn
            if with_sink:
                oa_ref[c, pos, :] = o
                continue
            lse = mx + jnp.log(dn)
            if first_group:
                obs_ref[c, rows, :] = o
                lse_ref[c, rows, :] = lse
            else:
                lse_old = lse_ref[c, rows, :]
                top = jnp.maximum(lse_old, lse)
                w_old = jnp.exp(lse_old - top)
                w_new = jnp.exp(lse - top)
                tot = w_old + w_new
                obs_ref[c, rows, :] = (w_old * obs_ref[c, rows, :] + w_new * o) / tot
                lse_ref[c, rows, :] = top + jnp.log(tot)

    def prep_step(b, carry):
        prep_stage(jnp.asarray(b, jnp.int32))
        return carry

    def attn_step(b, carry):
        b = jnp.asarray(b, jnp.int32)
        p, mxs = softmax_stage(score_stage(b))
        output_stage(b, value_stage(b, p), mxs)
        return carry

    lax.fori_loop(0, nb, prep_step, 0, unroll=BLOCK_UNROLL)
    cache_stage()
    lax.fori_loop(0, nb, attn_step, 0, unroll=BLOCK_UNROLL)


def _attn_prompt_kernel(sink_ref, z_ref, gq_ref, gk_ref, cos_ref, sa_ref, sb_ref, *rest, seq, n_alias):
    (oa_ref, ob_ref, *cache_refs, qs_ref, kt_ref, vs_ref, kn_ref, obs_ref, lse_ref, bias_ref) = rest[n_alias:]
    m = pl.program_id(1)
    qi = lax.broadcasted_iota(jnp.int32, (BAND, 2 * BAND), 0)
    kj = lax.broadcasted_iota(jnp.int32, (BAND, 2 * BAND), 1)
    in_band = kj <= qi + BAND
    bias_ref[0] = jnp.where(in_band & (kj >= qi), 0.0, NEG)
    bias_ref[1] = jnp.where(in_band & (kj >= BAND), 0.0, NEG)

    for idx, (_, d) in enumerate(MIXERS):
        @pl.when(m == idx)
        def _(idx=idx, d=d):
            _attn_mixer(d, idx, seq, sink_ref, z_ref, gq_ref, gk_ref, cos_ref, sa_ref, sb_ref,
                        oa_ref, cache_refs[idx], qs_ref, kt_ref, vs_ref, kn_ref, obs_ref, lse_ref, bias_ref)

    @pl.when(m == N_MIX - 1)
    def _():
        ob_ref[...] = obs_ref[...]


def _attn_prompt(layer, depth, sink, z4, gq, gk, cos, sa, sb, prev_caches):
    _, nb, seq, _ = z4.shape
    assert all(seq % (BAND * d) == 0 for _, d in MIXERS)
    per = ATT_W // LANES
    nq = QW // LANES
    tab = _const_spec((seq, LANES))
    gain = pl.BlockSpec((None, 1, LANES), lambda b, m: (m, 0, 0))
    o_spec = pl.BlockSpec((None, nq, seq, LANES), lambda b, m: (b, 0, 0, 0))
    slabs = pltpu.VMEM((nq, seq, LANES), F32)
    wins = [min(win, seq) for win, _ in MIXERS]
    n_alias = 0 if prev_caches is None else len(prev_caches)
    n_in = 7
    outs = pl.pallas_call(
        functools.partial(_attn_prompt_kernel, seq=seq, n_alias=n_alias),
        grid=(nb, N_MIX),
        in_specs=[pl.BlockSpec(memory_space=pltpu.SMEM),
                  pl.BlockSpec((per, None, seq, LANES), lambda b, m: (m, b, 0, 0)),
                  gain, gain, tab, tab, tab] + [pl.BlockSpec(memory_space=pl.ANY)] * n_alias,
        out_specs=[o_spec, o_spec]
                  + [pl.BlockSpec((None, None, 2, LANES, w), lambda b, m: (layer, b, 0, 0, 0)) for w in wins],
        out_shape=[jax.ShapeDtypeStruct((nb, nq, seq, LANES), F32),
                   jax.ShapeDtypeStruct((nb, nq, seq, LANES), F32)]
                  + [jax.ShapeDtypeStruct((depth, nb, 2, LANES, w), F32) for w in wins],
        scratch_shapes=[pltpu.VMEM((N_HEADS, seq, LANES), BF16),
                        pltpu.VMEM((seq // BAND, LANES, BAND), BF16),
                        pltpu.VMEM((seq, LANES), BF16),
                        pltpu.VMEM((seq, LANES), F32),
                        slabs, slabs,
                        pltpu.VMEM((2, BAND, 2 * BAND), F32)],
        input_output_aliases={n_in + j: 2 + j for j in range(n_alias)},
        compiler_params=_params(2),
        name="attn_prompt",
    )(sink, z4, gq, gk, cos, sa, sb, *(prev_caches or ()))
    return outs[0], outs[1], tuple(outs[2:])


def _column_norm_rope(x, g, cos, sin):
    ss = jnp.sum(x * x, axis=0, keepdims=True)
    xn = x * lax.rsqrt(ss * (1.0 / HEAD_DIM) + EPS) * g
    x1, x2 = xn[0:ROT_HALF], xn[ROT_HALF:ROT_DIM]
    return jnp.concatenate([x1 * cos - x2 * sin, x2 * cos + x1 * sin, xn[ROT_DIM:]], axis=0)


def _decode_kernel(sink_ref, zt_ref, gq_ref, gk_ref, cos_ref, sin_ref,
                   ca_ref, c1_ref, c2_ref, c3_ref, *rest, bb, n_alias):
    rest = rest[n_alias:]
    oat_ref, obt_ref, na_ref, n1_ref, n2_ref, n3_ref, qn_ref, kn_ref, vn_ref = rest
    i = pl.program_id(0)
    nbatch = zt_ref.shape[1]
    cache_refs = (ca_ref, c1_ref, c2_ref, c3_ref)
    new_refs = (na_ref, n1_ref, n2_ref, n3_ref)

    @pl.when(i == 0)
    def _():
        cos, sin = cos_ref[...], sin_ref[...]
        for m in range(N_MIX):
            for h in range(N_HEADS):
                r = m * ATT_W + h * HEAD_DIM
                o = m * QW + h * HEAD_DIM
                qn_ref[o:o + HEAD_DIM, :] = _column_norm_rope(
                    zt_ref[r:r + HEAD_DIM, :], gq_ref[m * HEAD_DIM:(m + 1) * HEAD_DIM, :], cos, sin
                ) * (1.0 / math.sqrt(HEAD_DIM))
            for kvh in range(N_KV):
                r = m * ATT_W + QW + kvh * HEAD_DIM
                o = m * KVW + kvh * HEAD_DIM
                kn_ref[o:o + HEAD_DIM, :] = _column_norm_rope(
                    zt_ref[r:r + HEAD_DIM, :], gk_ref[m * HEAD_DIM:(m + 1) * HEAD_DIM, :], cos, sin)
            vn_ref[m * KVW:(m + 1) * KVW, :] = zt_ref[m * ATT_W + QW + KVW:(m + 1) * ATT_W, :]
        oat_ref[...] = jnp.zeros_like(oat_ref)
        obt_ref[...] = jnp.zeros_like(obt_ref)

    lane_b = lax.broadcasted_iota(jnp.int32, (1, nbatch), 1)

    def per_head(x):
        return jnp.concatenate([jnp.broadcast_to(x[h:h + 1], (HEAD_DIM, 1)) for h in range(N_HEADS)], axis=0)

    def slot_rows(col):
        return jnp.concatenate([col[h * HEAD_DIM:(h + 1) * HEAD_DIM] for h in SLOT_HEADS], axis=0)

    def per_b(b):
        sel = lane_b == (i * bb + b)

        def column(ref):
            return jnp.sum(jnp.where(sel, ref[...], 0.0), axis=1, keepdims=True)

        qcol, kcol, vcol = column(qn_ref), column(kn_ref), column(vn_ref)
        run = None
        for m, (win, d) in enumerate(MIXERS):
            cref, nref = cache_refs[m], new_refs[m]
            last = lax.broadcasted_iota(jnp.int32, (HEAD_DIM, win), 1) == win - 1
            s_rows, new_rows, kv_new, vts = [], [], [], []
            for kvh in range(N_KV):
                kt, vt = cref[b, 0, kvh], cref[b, 1, kvh]
                r = m * KVW + kvh * HEAD_DIM
                kn, vn = kcol[r:r + HEAD_DIM], vcol[r:r + HEAD_DIM]
                nref[b, 0, kvh] = jnp.where(last, kn, pltpu.roll(kt, win - 1, 1))
                nref[b, 1, kvh] = jnp.where(last, vn, pltpu.roll(vt, win - 1, 1))
                for g in range(N_GROUP):
                    r = m * QW + SLOT_OF_HEAD[N_GROUP * kvh + g] * HEAD_DIM
                    qc = qcol[r:r + HEAD_DIM]
                    s_rows.append(jnp.sum(qc * kt, axis=0, keepdims=True))
                    new_rows.append(jnp.sum(qc * kn, axis=0, keepdims=True))
                vts.append(vt)
                kv_new.append(vn)
            s, s_new = jnp.concatenate(s_rows, axis=0), jnp.concatenate(new_rows, axis=0)
            if d > 1:
                s = jnp.where((lax.broadcasted_iota(jnp.int32, (1, win), 1) % d) == 0, s, NEG)
            mx = jnp.maximum(jnp.max(s, axis=1, keepdims=True), s_new)
            if m == 0:
                sink = jnp.concatenate([jnp.full((1, 1), sink_ref[h], F32) for h in range(N_HEADS)], axis=0)
                mx = jnp.maximum(mx, sink)
            p, p_new = jnp.exp(s - mx), jnp.exp(s_new - mx)
            l = jnp.sum(p, axis=1, keepdims=True) + p_new
            if m == 0:
                l = l + jnp.exp(sink - mx)
            folded = []
            for h in range(N_HEADS):
                pv = p[h:h + 1] * vts[h // N_GROUP]
                f = pv[:, 0:LANES]
                for j in range(1, win // LANES):
                    f = f + pv[:, j * LANES:(j + 1) * LANES]
                folded.append(f)
            acc = jnp.sum(jnp.concatenate(folded, axis=0), axis=1, keepdims=True)
            acc = acc + per_head(p_new) * jnp.concatenate([kv_new[h // N_GROUP] for h in range(N_HEADS)], axis=0)
            if m == 0:
                oat_ref[...] = jnp.where(sel, slot_rows(acc / per_head(l)), oat_ref[...])
            elif run is None:
                run = (mx, l, acc)
            else:
                mx0, l0, acc0 = run
                top = jnp.maximum(mx0, mx)
                a0, a1 = jnp.exp(mx0 - top), jnp.exp(mx - top)
                run = (top, a0 * l0 + a1 * l, per_head(a0) * acc0 + per_head(a1) * acc)
        _, l, acc = run
        obt_ref[...] = jnp.where(sel, slot_rows(acc / per_head(l)), obt_ref[...])

    for b in range(bb):
        per_b(b)


def _decode(layer, sink, zt, gq_col, gk_col, cos_col, sin_col, caches_t, prev_out, *, bb):
    nbatch = zt.shape[1]
    assert nbatch % bb == 0
    cache_specs = [pl.BlockSpec((None, bb) + c.shape[2:], lambda i: (layer, i, 0, 0, 0, 0)) for c in caches_t]
    n_alias = 0 if prev_out is None else len(prev_out)
    alias_specs = [pl.BlockSpec(memory_space=pl.ANY)] * n_alias
    n_in = 6 + len(caches_t)
    outs = pl.pallas_call(
        functools.partial(_decode_kernel, bb=bb, n_alias=n_alias),
        grid=(nbatch // bb,),
        in_specs=[pl.BlockSpec(memory_space=pltpu.SMEM), _const_spec(zt.shape),
                  _const_spec(gq_col.shape), _const_spec(gk_col.shape),
                  _const_spec(cos_col.shape), _const_spec(sin_col.shape)] + cache_specs + alias_specs,
        out_specs=[_const_spec((QW, nbatch)), _const_spec((QW, nbatch))] + cache_specs,
        out_shape=[jax.ShapeDtypeStruct((QW, nbatch), F32), jax.ShapeDtypeStruct((QW, nbatch), F32)]
                  + [jax.ShapeDtypeStruct(c.shape, F32) for c in caches_t],
        scratch_shapes=[pltpu.VMEM((N_MIX * QW, nbatch), F32), pltpu.VMEM((N_MIX * KVW, nbatch), F32),
                        pltpu.VMEM((N_MIX * KVW, nbatch), F32)],
        input_output_aliases={n_in + j: 2 + j for j in range(n_alias)},
        compiler_params=_params(1),
        name="decode_attn_cache",
    )(sink, zt, gq_col, gk_col, cos_col, sin_col, *caches_t, *(prev_out or ()))
    return outs[0], outs[1], tuple(outs[2:])


def _merge_kernel(x_ref, h_ref, oa_ref, ob_ref, oc_ref, od_ref, wg_ref, bg_ref, wb_ref, wo_ref, out_ref):
    h = h_ref[...]
    slabs = lambda ref: jnp.concatenate([ref[c] for c in range(ref.shape[0])], axis=1)
    branches = (slabs(oa_ref), slabs(ob_ref), oc_ref[...], od_ref[...])
    merged = None
    for n, br in enumerate(branches):
        gate = jax.nn.sigmoid(_dot(h, wg_ref[n]) + bg_ref[n:n + 1])
        t = gate * _dot(br.astype(BF16), wb_ref[n])
        merged = t if merged is None else merged + t
    out_ref[...] = x_ref[...] + _dot(merged.astype(BF16), wo_ref[...])


def _merge(layer, x2d, h, oa, ob, oc, od, wg, bg, wb, wo, *, tm):
    m_rows = x2d.shape[0]
    seq = oa.shape[2]
    assert m_rows % tm == 0 and seq % tm == 0
    tps = seq // tm
    row = lambda w: pl.BlockSpec((tm, w), lambda i: (i, 0))
    slab = pl.BlockSpec((None, QW // LANES, tm, LANES), lambda i: (i // tps, 0, i % tps, 0))
    return pl.pallas_call(
        _merge_kernel,
        grid=(m_rows // tm,),
        in_specs=[row(D_MODEL), row(D_MODEL), slab, slab, row(BRANCH_W), row(BRANCH_W)]
                 + [_layer_spec(wg, layer), _const_spec(bg.shape), _layer_spec(wb, layer), _layer_spec(wo, layer)],
        out_specs=row(D_MODEL),
        out_shape=jax.ShapeDtypeStruct((m_rows, D_MODEL), F32),
        compiler_params=_params(1),
        name="branch_merge",
    )(x2d, h, oa, ob, oc, od, wg, bg, wb, wo)


def _ffn_kernel(x_ref, ln_ref, wi_ref, wo_ref, out_ref, act_ref):
    x = x_ref[...]
    h = _rms(x, ln_ref[...]).astype(BF16)
    for j in range(D_FF // FF_CHUNK):
        cols = slice(j * FF_CHUNK, (j + 1) * FF_CHUNK)
        gt = _dot(h, wi_ref[:, cols])
        up = _dot(h, wi_ref[:, D_FF + j * FF_CHUNK:D_FF + (j + 1) * FF_CHUNK])
        act_ref[:, cols] = (gt * jax.nn.sigmoid(gt) * up).astype(BF16)
    out_ref[...] = x + _dot(act_ref[...], wo_ref[...])


def _ffn(layer, x2d, ln, wi, wo, *, tm):
    m_rows = x2d.shape[0]
    assert m_rows % tm == 0 and D_FF % FF_CHUNK == 0
    row = pl.BlockSpec((tm, D_MODEL), lambda i: (i, 0))
    return pl.pallas_call(
        _ffn_kernel,
        grid=(m_rows // tm,),
        in_specs=[row, _const_spec(ln.shape), _layer_spec(wi, layer), _layer_spec(wo, layer)],
        out_specs=row,
        out_shape=jax.ShapeDtypeStruct((m_rows, D_MODEL), F32),
        scratch_shapes=[pltpu.VMEM((tm, D_FF), BF16)],
        compiler_params=_params(1),
        name="swiglu_ffn",
    )(x2d, ln, wi, wo)


def _rope_angles(pos):
    inv_freq = ROPE_THETA ** (-2.0 * jnp.arange(ROT_HALF, dtype=F32) / ROT_DIM)
    ang = pos[:, None] * inv_freq[None, :]
    return jnp.cos(ang), jnp.sin(ang)


def _rope_row_tables(pos):
    cos, sin = _rope_angles(pos)
    n = pos.shape[0]
    zero = jnp.zeros((n, ROT_HALF), F32)
    rest0 = jnp.zeros((n, HEAD_DIM - ROT_DIM), F32)
    c = jnp.concatenate([cos, cos, jnp.ones((n, HEAD_DIM - ROT_DIM), F32)], axis=1)
    sa = jnp.concatenate([-sin, zero, rest0], axis=1)
    sb = jnp.concatenate([zero, sin, rest0], axis=1)
    return tuple(jnp.tile(t, (1, N_KV)) for t in (c, sa, sb))


def kernel(x_prompt, x_sample, cache_a, cache_b1, cache_b2, cache_b3, state_c, ln1, w_in, q_norm_a, k_norm_a,
           sink_a, q_norm_b, k_norm_b, conv_c, ws_d, bs_d, w_gate, b_gate, w_branch, w_o, ln2, w_ffn_in,
           w_ffn_out):
    bp, sp, _ = x_prompt.shape
    bs_n, ds, _ = x_sample.shape
    depth = ln1.shape[0]
    assert ds == 1 and sp >= CONV_W - 1
    caches = (cache_a, cache_b1, cache_b2, cache_b3)
    for c, (win, d) in zip(caches, MIXERS):
        assert c.shape[3] == win and win // d == BAND

    cos_p, sa_p, sb_p = _rope_row_tables(jnp.arange(sp, dtype=F32))
    cos_s, sin_s = _rope_angles(PAST_LEN + jnp.arange(ds, dtype=F32))
    cos_col, sin_col = cos_s.reshape(ROT_HALF, 1), sin_s.reshape(ROT_HALF, 1)
    caches_t = tuple(jnp.transpose(c, (0, 1, 2, 4, 5, 3)) for c in caches)

    y_p = x_prompt.reshape(bp * sp, D_MODEL)
    y_s = x_sample.reshape(bs_n, D_MODEL)
    c_p, c_s, d_s = [], [], []
    kv_p = new_t = None
    slot_cols = [h * HEAD_DIM + e for h in SLOT_HEADS for e in range(HEAD_DIM)]
    in_cols = jnp.asarray([m * ATT_W + c for m in range(N_MIX) for c in slot_cols + list(range(QW, ATT_W))]
                          + list(range(N_ATT, N_IN)), jnp.int32)
    slot_rows = jnp.asarray(slot_cols, jnp.int32)
    w_in_b = w_in.astype(BF16)[:, :, in_cols]
    wb_b = jnp.concatenate([w_branch[:, :2][:, :, slot_rows], w_branch[:, 2:]], axis=1).astype(BF16)
    wg_b, wo_b = w_gate.astype(BF16), w_o.astype(BF16)
    wfi_b, wfo_b = w_ffn_in.astype(BF16), w_ffn_out.astype(BF16)
    for l in range(depth):
        ln1_l, ln2_l = ln1[l][None], ln2[l][None]
        gq_all = jnp.concatenate([q_norm_a[l][None], q_norm_b[l]], axis=0)
        gk_all = jnp.concatenate([k_norm_a[l][None], k_norm_b[l]], axis=0)
        bst = jnp.repeat(bs_d[l].T, HEAD_DIM, axis=1)

        h, zatt, oc, od, newc = _inproj_prompt(l, y_p, ln1_l, w_in_b, conv_c[l], ws_d[l], bst, seq=sp, tm=ROW_TILE)
        oa, ob, kv_p = _attn_prompt(l, depth, sink_a[l], zatt.reshape(N_ATT // LANES, bp, sp, LANES),
                                    jnp.tile(gq_all, (1, N_KV))[:, None], jnp.tile(gk_all, (1, N_KV))[:, None],
                                    cos_p, sa_p, sb_p, kv_p)
        x1 = _merge(l, y_p, h, oa, ob, oc, od, wg_b, b_gate[l], wb_b, wo_b, tm=ROW_TILE)
        y_p = _ffn(l, x1, ln2_l, wfi_b, wfo_b, tm=ROW_TILE)
        c_p.append(newc)

        s0, s1 = state_c[l][:, 0], state_c[l][:, 1]
        w00 = jnp.repeat(ws_d[l][:, 0, 0], HEAD_DIM)[None]
        b0 = jnp.repeat(bs_d[l][:, 0], HEAD_DIM)[None]
        h, zatt, oc, od, zc, vd = _inproj_sample(l, y_s, ln1_l, w_in_b, conv_c[l], s0, s1, w00, b0)
        oat, obt, new_t = _decode(l, sink_a[l], zatt.T, gq_all.reshape(-1, 1), gk_all.reshape(-1, 1),
                                  cos_col, sin_col, caches_t, new_t, bb=DEC_BLOCK)
        to_slabs = lambda t: t.reshape(QW // LANES, LANES, bs_n).transpose(0, 2, 1)[None]
        x1 = _merge(l, y_s, h, to_slabs(oat), to_slabs(obt), oc, od, wg_b, b_gate[l], wb_b, wo_b, tm=bs_n)
        y_s = _ffn(l, x1, ln2_l, wfi_b, wfo_b, tm=bs_n)
        c_s.append(jnp.stack([s1, zc], axis=1))
        d_s.append(vd[:, None])

    def prompt_cache(m):
        t = kv_p[m]
        return jnp.transpose(t.reshape(depth, bp, 2, N_KV, HEAD_DIM, t.shape[-1]), (0, 1, 2, 5, 3, 4))

    sample_cache = [jnp.transpose(t, (0, 1, 2, 5, 3, 4)) for t in new_t]
    return (y_p.reshape(bp, sp, D_MODEL), y_s.reshape(bs_n, ds, D_MODEL),
            prompt_cache(0), sample_cache[0], prompt_cache(1), sample_cache[1],
            prompt_cache(2), sample_cache[2], prompt_cache(3), sample_cache[3],
            jnp.stack(c_p), jnp.stack(c_s), jnp.stack(d_s))
```

```python
import functools
import math

import jax
import jax.numpy as jnp
from jax import lax
from jax.experimental import pallas as pl
from jax.experimental.pallas import tpu as pltpu

F32 = jnp.float32
BF16 = jnp.bfloat16

D_MODEL = 1024
HEAD_DIM = 64
ROT_DIM = HEAD_DIM // 4
ROT_HALF = ROT_DIM // 2
ROPE_THETA = 500000.0
BRANCH_W = 384
N_HEADS = BRANCH_W // HEAD_DIM
N_KV = 2
N_GROUP = N_HEADS // N_KV
QW = N_HEADS * HEAD_DIM
KVW = N_KV * HEAD_DIM
ATT_W = QW + 2 * KVW
WIN_A = 128
DIL_GROUPS = ((128, 1), (512, 4), (2048, 16))
MIXERS = ((WIN_A, 1),) + DIL_GROUPS
N_MIX = len(MIXERS)
BAND = 128
LANES = 128
SLOT_HEADS = (0, 3, 1, 4, 2, 5)
SLOT_OF_HEAD = tuple(SLOT_HEADS.index(h) for h in range(6))
CONV_W = 3
CHUNK = 128
N_SG = 6
N_BRANCH = 4
N_ATT = ATT_W * N_MIX
N_IN = N_ATT + 5 * BRANCH_W
D_FF = 2816
FF_CHUNK = 256
PAST_LEN = 16384
EPS = 1e-6
NEG = -1e30

VMEM_LIMIT_V7X = 56 * 1024 * 1024
ROW_TILE = 512
DEC_BLOCK = 2
BLOCK_UNROLL = 8
PROJ_CHUNK = 1280


def _params(n_axes, vmem=VMEM_LIMIT_V7X):
    return pltpu.CompilerParams(dimension_semantics=("arbitrary",) * n_axes, vmem_limit_bytes=vmem)


def _const_spec(shape):
    nd = len(shape)
    return pl.BlockSpec(shape, lambda *_: (0,) * nd)


def _layer_spec(stacked, layer):
    shape = stacked.shape[1:]
    return pl.BlockSpec((None,) + shape, lambda *_: (layer,) + (0,) * len(shape))


def _rms(x, g):
    return x * lax.rsqrt(jnp.mean(x * x, axis=-1, keepdims=True) + EPS) * g


def _dot(a, b):
    return jnp.dot(a, b, preferred_element_type=F32)


def _project(x_ref, ln_ref, w_ref, h_ref, zatt_ref, slabs):
    h = _rms(x_ref[...], ln_ref[...]).astype(BF16)
    h_ref[...] = h
    pieces = []
    for lo in range(0, N_IN, PROJ_CHUNK):
        hi = min(lo + PROJ_CHUNK, N_IN)
        z = _dot(h, w_ref[:, lo:hi])
        pieces += [z[:, c * LANES:(c + 1) * LANES] for c in range((hi - lo) // LANES)]
    n_att = N_ATT // LANES
    if slabs:
        for c in range(n_att):
            zatt_ref[c] = pieces[c]
    else:
        zatt_ref[...] = jnp.concatenate(pieces[:n_att], axis=1)
    per = BRANCH_W // LANES
    return [jnp.concatenate(pieces[n_att + j * per:n_att + (j + 1) * per], axis=1) for j in range(5)]


def _inproj_prompt_kernel(x_ref, ln_ref, w_ref, conv_ref, ws_ref, bst_ref,
                          h_ref, zatt_ref, oc_ref, od_ref, newc_ref, carry_ref, *, tm, tiles_per_seq):
    i = pl.program_id(0)
    gate_b, gate_c, x_c, u_d, v_d = _project(x_ref, ln_ref, w_ref, h_ref, zatt_ref, True)

    @pl.when(i % tiles_per_seq == 0)
    def _():
        carry_ref[...] = jnp.zeros_like(carry_ref)

    zc = gate_c * x_c
    carry = carry_ref[...]
    row = lax.broadcasted_iota(jnp.int32, zc.shape, 0)
    z1 = jnp.where(row == 0, carry[1:2], pltpu.roll(zc, 1, 0))
    z2 = jnp.where(row == 0, carry[0:1], jnp.where(row == 1, carry[1:2], pltpu.roll(zc, 2, 0)))
    conv = conv_ref[0:1] * z2 + conv_ref[1:2] * z1 + conv_ref[2:3] * zc
    oc_ref[...] = gate_b * conv
    tail = zc[tm - 2:tm]
    carry_ref[...] = tail
    newc_ref[...] = tail

    tri = (lax.broadcasted_iota(jnp.int32, (CHUNK, CHUNK), 0)
           >= lax.broadcasted_iota(jnp.int32, (CHUNK, CHUNK), 1))
    vb = v_d.astype(BF16)
    nchunk = tm // CHUNK
    mixed = []
    for g in range(N_SG):
        wm = jnp.where(tri, ws_ref[g], 0.0).astype(BF16)
        vg = jnp.concatenate([vb[c * CHUNK:(c + 1) * CHUNK, g * HEAD_DIM:(g + 1) * HEAD_DIM]
                              for c in range(nchunk)], axis=1)
        mixed.append(_dot(wm, vg))
    for c in range(nchunk):
        rows = slice(c * CHUNK, (c + 1) * CHUNK)
        sv = jnp.concatenate([mixed[g][:, c * HEAD_DIM:(c + 1) * HEAD_DIM] for g in range(N_SG)], axis=1)
        od_ref[rows, :] = u_d[rows] * (sv + bst_ref[...])


def _inproj_sample_kernel(x_ref, ln_ref, w_ref, conv_ref, s0_ref, s1_ref, w00_ref, b0_ref,
                          h_ref, zatt_ref, oc_ref, od_ref, zc_ref, vd_ref):
    gate_b, gate_c, x_c, u_d, v_d = _project(x_ref, ln_ref, w_ref, h_ref, zatt_ref, False)
    zc = gate_c * x_c
    conv = conv_ref[0:1] * s0_ref[...] + conv_ref[1:2] * s1_ref[...] + conv_ref[2:3] * zc
    oc_ref[...] = gate_b * conv
    zc_ref[...] = zc
    od_ref[...] = u_d * (v_d * w00_ref[...] + b0_ref[...])
    vd_ref[...] = v_d


def _inproj_prompt(layer, x2d, ln, w_bf, conv, ws, bst, *, seq, tm):
    m_rows = x2d.shape[0]
    assert m_rows % seq == 0 and seq % tm == 0 and tm % CHUNK == 0
    tps = seq // tm
    row = lambda w: pl.BlockSpec((tm, w), lambda i: (i, 0))
    return pl.pallas_call(
        functools.partial(_inproj_prompt_kernel, tm=tm, tiles_per_seq=tps),
        grid=(m_rows // tm,),
        in_specs=[row(D_MODEL), _const_spec((1, D_MODEL)), _layer_spec(w_bf, layer),
                  _const_spec((CONV_W, BRANCH_W)), _const_spec((N_SG, CHUNK, CHUNK)),
                  _const_spec((CHUNK, BRANCH_W))],
        out_specs=[row(D_MODEL), pl.BlockSpec((N_ATT // LANES, tm, LANES), lambda i: (0, i, 0)),
                   row(BRANCH_W), row(BRANCH_W),
                   pl.BlockSpec((None, CONV_W - 1, BRANCH_W), lambda i: (i // tps, 0, 0))],
        out_shape=[jax.ShapeDtypeStruct((m_rows, D_MODEL), BF16),
                   jax.ShapeDtypeStruct((N_ATT // LANES, m_rows, LANES), F32),
                   jax.ShapeDtypeStruct((m_rows, BRANCH_W), F32),
                   jax.ShapeDtypeStruct((m_rows, BRANCH_W), F32),
                   jax.ShapeDtypeStruct((m_rows // seq, CONV_W - 1, BRANCH_W), F32)],
        scratch_shapes=[pltpu.VMEM((CONV_W - 1, BRANCH_W), F32)],
        compiler_params=_params(1),
        name="inproj_prompt",
    )(x2d, ln, w_bf, conv, ws, bst)


def _inproj_sample(layer, x2d, ln, w_bf, conv, s0, s1, w00, b0):
    n = x2d.shape[0]
    full = lambda w: _const_spec((n, w))
    return pl.pallas_call(
        _inproj_sample_kernel,
        grid=(1,),
        in_specs=[full(D_MODEL), _const_spec((1, D_MODEL)), _layer_spec(w_bf, layer),
                  _const_spec((CONV_W, BRANCH_W)), full(BRANCH_W), full(BRANCH_W),
                  _const_spec((1, BRANCH_W)), _const_spec((1, BRANCH_W))],
        out_specs=[full(D_MODEL), full(N_ATT)] + [full(BRANCH_W)] * 4,
        out_shape=[jax.ShapeDtypeStruct((n, D_MODEL), BF16), jax.ShapeDtypeStruct((n, N_ATT), F32)]
                  + [jax.ShapeDtypeStruct((n, BRANCH_W), F32)] * 4,
        compiler_params=_params(1),
        name="inproj_sample",
    )(x2d, ln, w_bf, conv, s0, s1, w00, b0)


def _segment_ones(n):
    r = lax.broadcasted_iota(jnp.int32, (n, n), 0) // HEAD_DIM
    c = lax.broadcasted_iota(jnp.int32, (n, n), 1) // HEAD_DIM
    return (r == c).astype(BF16)


def _rotary_partner_matrix():
    j = lax.broadcasted_iota(jnp.int32, (LANES, LANES), 0)
    l = lax.broadcasted_iota(jnp.int32, (LANES, LANES), 1)
    e = l % HEAD_DIM
    first, second = e < ROT_HALF, (e >= ROT_HALF) & (e < ROT_DIM)
    return ((first & (j == l + ROT_HALF)) | (second & (j == l - ROT_HALF))).astype(BF16)


def _head_norm_rope(xs, gains, ones, partner, cos, sin):
    rows = xs[0].shape[0]
    ss = _dot(jnp.concatenate([(x * x).astype(BF16) for x in xs], axis=0), ones)
    xn = [x * lax.rsqrt(ss[c * rows:(c + 1) * rows] * (1.0 / HEAD_DIM) + EPS) * g
          for c, (x, g) in enumerate(zip(xs, gains))]
    xr = _dot(jnp.concatenate([t.astype(BF16) for t in xn], axis=0), partner)
    return [t * cos + xr[c * rows:(c + 1) * rows] * sin for c, t in enumerate(xn)]


def _attn_mixer(d, idx, seq, sink_ref, z_ref, gq_ref, gk_ref, cos_ref, sin_ref,
                oa_ref, cache_ref, qs_ref, kt_ref, vs_ref, kn_ref, obs_ref, lse_ref, bias_ref):
    with_sink, first_group = idx == 0, idx == 1
    nblk = seq // (BAND * d)
    has_prev = nblk > 1
    shift = nblk.bit_length() - 1
    ones = _segment_ones(LANES)
    partner = _rotary_partner_matrix()
    low = lax.broadcasted_iota(jnp.int32, (BAND, LANES), 1) < HEAD_DIM

    def natural_rows(it):
        if d == 1:
            return pl.ds(pl.multiple_of(it * BAND, BAND), BAND)
        return pl.ds((it >> shift) + (it & (nblk - 1)) * (BAND * d), BAND, stride=d)

    nb = seq // BAND
    nk = 2 * BAND if has_prev else BAND
    nq = QW // LANES
    ones_cols = jnp.ones((nk, LANES), BF16)

    def block_pos(b):
        return pl.ds(pl.multiple_of(b * BAND, BAND), BAND)

    def prev_block(b):
        return jnp.maximum(b - 1, 0)

    def prep_stage(it):
        rows = natural_rows(it)
        pos = block_pos(it)
        *qn, kn = _head_norm_rope([z_ref[c, rows, :] for c in range(nq + 1)],
                                  [gq_ref[...]] * nq + [gk_ref[...]], ones, partner,
                                  cos_ref[rows, :], sin_ref[rows, :])
        for c in range(nq):
            qc = (qn[c] * (1.0 / math.sqrt(HEAD_DIM))).astype(BF16)
            qs_ref[2 * c, pos, :] = jnp.where(low, qc, jnp.zeros_like(qc))
            qs_ref[2 * c + 1, pos, :] = jnp.where(low, jnp.zeros_like(qc), qc)
        kn_ref[rows, :] = kn
        kt_ref[it] = kn.T.astype(BF16)
        vs_ref[pos, :] = z_ref[nq + 1, rows, :].astype(BF16)

    def cache_stage():
        win = cache_ref.shape[2]
        for j in range(win // BAND):
            t0 = seq - win + j * BAND
            cache_ref[0, :, j * BAND:(j + 1) * BAND] = kn_ref[t0:t0 + BAND, :].T
            cache_ref[1, :, j * BAND:(j + 1) * BAND] = z_ref[nq + 1, t0:t0 + BAND, :].T

    def score_stage(b):
        q6 = jnp.concatenate([qs_ref[j, block_pos(b), :] for j in range(N_HEADS)], axis=0)
        if has_prev:
            first = (b & (nblk - 1)) == 0
            kt = jnp.concatenate([kt_ref[prev_block(b)], kt_ref[b]], axis=1)
            bias = bias_ref[jnp.where(first, 1, 0)]
        else:
            kt = kt_ref[b]
            bias = bias_ref[1, :, BAND:]
        return _dot(q6, kt) + jnp.concatenate([bias] * N_HEADS, axis=0)

    def softmax_stage(s):
        mx = jnp.max(s, axis=1, keepdims=True)
        p = jnp.exp(s - mx).astype(BF16)
        return p, [jnp.where(low, mx[2 * c * BAND:(2 * c + 1) * BAND], mx[(2 * c + 1) * BAND:(2 * c + 2) * BAND])
                   for c in range(nq)]

    def value_stage(b, p):
        v = vs_ref[block_pos(b), :]
        if has_prev:
            v = jnp.concatenate([vs_ref[block_pos(prev_block(b)), :], v], axis=0)
        v_aug = jnp.concatenate([v, ones_cols], axis=1)
        return _dot(p, v_aug)

    def output_stage(b, r2, mxs):
        pos = block_pos(b)
        rows = natural_rows(b)
        for c in range(nq):
            ra, rb = slice(2 * c * BAND, (2 * c + 1) * BAND), slice((2 * c + 1) * BAND, (2 * c + 2) * BAND)
            dn = jnp.where(low, r2[ra, LANES:], r2[rb, LANES:])
            mx = mxs[c]
            if with_sink:
                dn = dn + jnp.exp(jnp.where(low, sink_ref[SLOT_HEADS[2 * c]], sink_ref[SLOT_HEADS[2 * c + 1]]) - mx)
            o = jnp.where(low, r2[ra, :LANES], r2[rb, :LANES]) / dn
            if with_sink:
                oa_ref[c, pos, :] = o
                continue
            lse = mx + jnp.log(dn)
            if first_group:
                obs_ref[c, rows, :] = o
                lse_ref[c, rows, :] = lse
            else:
                lse_old = lse_ref[c, rows, :]
                top = jnp.maximum(lse_old, lse)
                w_old = jnp.exp(lse_old - top)
                w_new = jnp.exp(lse - top)
                tot = w_old + w_new
                obs_ref[c, rows, :] = (w_old * obs_ref[c, rows, :] + w_new * o) / tot
                lse_ref[c, rows, :] = top + jnp.log(tot)

    def prep_step(b, carry):
        prep_stage(jnp.asarray(b, jnp.int32))
        return carry

    def attn_step(b, carry):
        b = jnp.asarray(b, jnp.int32)
        p, mxs = softmax_stage(score_stage(b))
        output_stage(b, value_stage(b, p), mxs)
        return carry

    lax.fori_loop(0, nb, prep_step, 0, unroll=BLOCK_UNROLL)
    cache_stage()
    lax.fori_loop(0, nb, attn_step, 0, unroll=BLOCK_UNROLL)


def _attn_prompt_kernel(sink_ref, z_ref, gq_ref, gk_ref, cos_ref, sin_ref, *rest, seq, n_alias):
    (oa_ref, ob_ref, *cache_refs, qs_ref, kt_ref, vs_ref, kn_ref, obs_ref, lse_ref, bias_ref) = rest[n_alias:]
    m = pl.program_id(1)
    qi = lax.broadcasted_iota(jnp.int32, (BAND, 2 * BAND), 0)
    kj = lax.broadcasted_iota(jnp.int32, (BAND, 2 * BAND), 1)
    in_band = kj <= qi + BAND
    bias_ref[0] = jnp.where(in_band & (kj >= qi), 0.0, NEG)
    bias_ref[1] = jnp.where(in_band & (kj >= BAND), 0.0, NEG)

    for idx, (_, d) in enumerate(MIXERS):
        @pl.when(m == idx)
        def _(idx=idx, d=d):
            _attn_mixer(d, idx, seq, sink_ref, z_ref, gq_ref, gk_ref, cos_ref, sin_ref,
                        oa_ref, cache_refs[idx], qs_ref, kt_ref, vs_ref, kn_ref, obs_ref, lse_ref, bias_ref)

    @pl.when(m == N_MIX - 1)
    def _():
        ob_ref[...] = obs_ref[...]


def _attn_prompt(layer, depth, sink, z4, gq, gk, cos, sin, prev_caches):
    _, nb, seq, _ = z4.shape
    assert all(seq % (BAND * d) == 0 for _, d in MIXERS)
    per = ATT_W // LANES
    nq = QW // LANES
    tab = _const_spec((seq, LANES))
    gain = pl.BlockSpec((None, 1, LANES), lambda b, m: (m, 0, 0))
    o_spec = pl.BlockSpec((None, nq, seq, LANES), lambda b, m: (b, 0, 0, 0))
    slabs = pltpu.VMEM((nq, seq, LANES), F32)
    wins = [min(win, seq) for win, _ in MIXERS]
    n_alias = 0 if prev_caches is None else len(prev_caches)
    n_in = 6
    outs = pl.pallas_call(
        functools.partial(_attn_prompt_kernel, seq=seq, n_alias=n_alias),
        grid=(nb, N_MIX),
        in_specs=[pl.BlockSpec(memory_space=pltpu.SMEM),
                  pl.BlockSpec((per, None, seq, LANES), lambda b, m: (m, b, 0, 0)),
                  gain, gain, tab, tab] + [pl.BlockSpec(memory_space=pl.ANY)] * n_alias,
        out_specs=[o_spec, o_spec]
                  + [pl.BlockSpec((None, None, 2, LANES, w), lambda b, m: (layer, b, 0, 0, 0)) for w in wins],
        out_shape=[jax.ShapeDtypeStruct((nb, nq, seq, LANES), F32),
                   jax.ShapeDtypeStruct((nb, nq, seq, LANES), F32)]
                  + [jax.ShapeDtypeStruct((depth, nb, 2, LANES, w), F32) for w in wins],
        scratch_shapes=[pltpu.VMEM((N_HEADS, seq, LANES), BF16),
                        pltpu.VMEM((seq // BAND, LANES, BAND), BF16),
                        pltpu.VMEM((seq, LANES), BF16),
                        pltpu.VMEM((seq, LANES), F32),
                        slabs, slabs,
                        pltpu.VMEM((2, BAND, 2 * BAND), F32)],
        input_output_aliases={n_in + j: 2 + j for j in range(n_alias)},
        compiler_params=_params(2),
        name="attn_prompt",
    )(sink, z4, gq, gk, cos, sin, *(prev_caches or ()))
    return outs[0], outs[1], tuple(outs[2:])


def _column_norm_rope(x, g, cos, sin):
    ss = jnp.sum(x * x, axis=0, keepdims=True)
    xn = x * lax.rsqrt(ss * (1.0 / HEAD_DIM) + EPS) * g
    x1, x2 = xn[0:ROT_HALF], xn[ROT_HALF:ROT_DIM]
    return jnp.concatenate([x1 * cos - x2 * sin, x2 * cos + x1 * sin, xn[ROT_DIM:]], axis=0)


def _decode_kernel(sink_ref, zt_ref, gq_ref, gk_ref, cos_ref, sin_ref,
                   ca_ref, c1_ref, c2_ref, c3_ref, *rest, bb, n_alias):
    rest = rest[n_alias:]
    oat_ref, obt_ref, na_ref, n1_ref, n2_ref, n3_ref, qn_ref, kn_ref, vn_ref = rest
    i = pl.program_id(0)
    nbatch = zt_ref.shape[1]
    cache_refs = (ca_ref, c1_ref, c2_ref, c3_ref)
    new_refs = (na_ref, n1_ref, n2_ref, n3_ref)

    @pl.when(i == 0)
    def _():
        cos, sin = cos_ref[...], sin_ref[...]
        for m in range(N_MIX):
            for h in range(N_HEADS):
                r = m * ATT_W + h * HEAD_DIM
                o = m * QW + h * HEAD_DIM
                qn_ref[o:o + HEAD_DIM, :] = _column_norm_rope(
                    zt_ref[r:r + HEAD_DIM, :], gq_ref[m * HEAD_DIM:(m + 1) * HEAD_DIM, :], cos, sin
                ) * (1.0 / math.sqrt(HEAD_DIM))
            for kvh in range(N_KV):
                r = m * ATT_W + QW + kvh * HEAD_DIM
                o = m * KVW + kvh * HEAD_DIM
                kn_ref[o:o + HEAD_DIM, :] = _column_norm_rope(
                    zt_ref[r:r + HEAD_DIM, :], gk_ref[m * HEAD_DIM:(m + 1) * HEAD_DIM, :], cos, sin)
            vn_ref[m * KVW:(m + 1) * KVW, :] = zt_ref[m * ATT_W + QW + KVW:(m + 1) * ATT_W, :]
        oat_ref[...] = jnp.zeros_like(oat_ref)
        obt_ref[...] = jnp.zeros_like(obt_ref)

    lane_b = lax.broadcasted_iota(jnp.int32, (1, nbatch), 1)

    def per_head(x):
        return jnp.concatenate([jnp.broadcast_to(x[h:h + 1], (HEAD_DIM, 1)) for h in range(N_HEADS)], axis=0)

    def slot_rows(col):
        return jnp.concatenate([col[h * HEAD_DIM:(h + 1) * HEAD_DIM] for h in SLOT_HEADS], axis=0)

    def per_b(b):
        sel = lane_b == (i * bb + b)

        def column(ref):
            return jnp.sum(jnp.where(sel, ref[...], 0.0), axis=1, keepdims=True)

        qcol, kcol, vcol = column(qn_ref), column(kn_ref), column(vn_ref)
        run = None
        for m, (win, d) in enumerate(MIXERS):
            cref, nref = cache_refs[m], new_refs[m]
            last = lax.broadcasted_iota(jnp.int32, (HEAD_DIM, win), 1) == win - 1
            s_rows, new_rows, kv_new, vts = [], [], [], []
            for kvh in range(N_KV):
                kt, vt = cref[b, 0, kvh], cref[b, 1, kvh]
                r = m * KVW + kvh * HEAD_DIM
                kn, vn = kcol[r:r + HEAD_DIM], vcol[r:r + HEAD_DIM]
                nref[b, 0, kvh] = jnp.where(last, kn, pltpu.roll(kt, win - 1, 1))
                nref[b, 1, kvh] = jnp.where(last, vn, pltpu.roll(vt, win - 1, 1))
                for g in range(N_GROUP):
                    r = m * QW + SLOT_OF_HEAD[N_GROUP * kvh + g] * HEAD_DIM
                    qc = qcol[r:r + HEAD_DIM]
                    s_rows.append(jnp.sum(qc * kt, axis=0, keepdims=True))
                    new_rows.append(jnp.sum(qc * kn, axis=0, keepdims=True))
                vts.append(vt)
                kv_new.append(vn)
            s, s_new = jnp.concatenate(s_rows, axis=0), jnp.concatenate(new_rows, axis=0)
            if d > 1:
                s = jnp.where((lax.broadcasted_iota(jnp.int32, (1, win), 1) % d) == 0, s, NEG)
            mx = jnp.maximum(jnp.max(s, axis=1, keepdims=True), s_new)
            if m == 0:
                sink = jnp.concatenate([jnp.full((1, 1), sink_ref[h], F32) for h in range(N_HEADS)], axis=0)
                mx = jnp.maximum(mx, sink)
            p, p_new = jnp.exp(s - mx), jnp.exp(s_new - mx)
            l = jnp.sum(p, axis=1, keepdims=True) + p_new
            if m == 0:
                l = l + jnp.exp(sink - mx)
            folded = []
            for h in range(N_HEADS):
                pv = p[h:h + 1] * vts[h // N_GROUP]
                f = pv[:, 0:LANES]
                for j in range(1, win // LANES):
                    f = f + pv[:, j * LANES:(j + 1) * LANES]
                folded.append(f)
            acc = jnp.sum(jnp.concatenate(folded, axis=0), axis=1, keepdims=True)
            acc = acc + per_head(p_new) * jnp.concatenate([kv_new[h // N_GROUP] for h in range(N_HEADS)], axis=0)
            if m == 0:
                oat_ref[...] = jnp.where(sel, slot_rows(acc / per_head(l)), oat_ref[...])
            elif run is None:
                run = (mx, l, acc)
            else:
                mx0, l0, acc0 = run
                top = jnp.maximum(mx0, mx)
                a0, a1 = jnp.exp(mx0 - top), jnp.exp(mx - top)
                run = (top, a0 * l0 + a1 * l, per_head(a0) * acc0 + per_head(a1) * acc)
        _, l, acc = run
        obt_ref[...] = jnp.where(sel, slot_rows(acc / per_head(l)), obt_ref[...])

    for b in range(bb):
        per_b(b)


def _decode(layer, sink, zt, gq_col, gk_col, cos_col, sin_col, caches_t, prev_out, *, bb):
    nbatch = zt.shape[1]
    assert nbatch % bb == 0
    cache_specs = [pl.BlockSpec((None, bb) + c.shape[2:], lambda i: (layer, i, 0, 0, 0, 0)) for c in caches_t]
    n_alias = 0 if prev_out is None else len(prev_out)
    alias_specs = [pl.BlockSpec(memory_space=pl.ANY)] * n_alias
    n_in = 6 + len(caches_t)
    outs = pl.pallas_call(
        functools.partial(_decode_kernel, bb=bb, n_alias=n_alias),
        grid=(nbatch // bb,),
        in_specs=[pl.BlockSpec(memory_space=pltpu.SMEM), _const_spec(zt.shape),
                  _const_spec(gq_col.shape), _const_spec(gk_col.shape),
                  _const_spec(cos_col.shape), _const_spec(sin_col.shape)] + cache_specs + alias_specs,
        out_specs=[_const_spec((QW, nbatch)), _const_spec((QW, nbatch))] + cache_specs,
        out_shape=[jax.ShapeDtypeStruct((QW, nbatch), F32), jax.ShapeDtypeStruct((QW, nbatch), F32)]
                  + [jax.ShapeDtypeStruct(c.shape, F32) for c in caches_t],
        scratch_shapes=[pltpu.VMEM((N_MIX * QW, nbatch), F32), pltpu.VMEM((N_MIX * KVW, nbatch), F32),
                        pltpu.VMEM((N_MIX * KVW, nbatch), F32)],
        input_output_aliases={n_in + j: 2 + j for j in range(n_alias)},
        compiler_params=_params(1),
        name="decode_attn_cache",
    )(sink, zt, gq_col, gk_col, cos_col, sin_col, *caches_t, *(prev_out or ()))
    return outs[0], outs[1], tuple(outs[2:])


def _merge_kernel(x_ref, h_ref, oa_ref, ob_ref, oc_ref, od_ref, wg_ref, bg_ref, wb_ref, wo_ref, out_ref):
    h = h_ref[...]
    slabs = lambda ref: jnp.concatenate([ref[c] for c in range(ref.shape[0])], axis=1)
    branches = (slabs(oa_ref), slabs(ob_ref), oc_ref[...], od_ref[...])
    merged = None
    for n, br in enumerate(branches):
        gate = jax.nn.sigmoid(_dot(h, wg_ref[n]) + bg_ref[n:n + 1])
        t = gate * _dot(br.astype(BF16), wb_ref[n])
        merged = t if merged is None else merged + t
    out_ref[...] = x_ref[...] + _dot(merged.astype(BF16), wo_ref[...])


def _merge(layer, x2d, h, oa, ob, oc, od, wg, bg, wb, wo, *, tm):
    m_rows = x2d.shape[0]
    seq = oa.shape[2]
    assert m_rows % tm == 0 and seq % tm == 0
    tps = seq // tm
    row = lambda w: pl.BlockSpec((tm, w), lambda i: (i, 0))
    slab = pl.BlockSpec((None, QW // LANES, tm, LANES), lambda i: (i // tps, 0, i % tps, 0))
    return pl.pallas_call(
        _merge_kernel,
        grid=(m_rows // tm,),
        in_specs=[row(D_MODEL), row(D_MODEL), slab, slab, row(BRANCH_W), row(BRANCH_W)]
                 + [_layer_spec(wg, layer), _const_spec(bg.shape), _layer_spec(wb, layer), _layer_spec(wo, layer)],
        out_specs=row(D_MODEL),
        out_shape=jax.ShapeDtypeStruct((m_rows, D_MODEL), F32),
        compiler_params=_params(1),
        name="branch_merge",
    )(x2d, h, oa, ob, oc, od, wg, bg, wb, wo)


def _ffn_kernel(x_ref, ln_ref, wi_ref, wo_ref, out_ref, act_ref):
    x = x_ref[...]
    h = _rms(x, ln_ref[...]).astype(BF16)
    for j in range(D_FF // FF_CHUNK):
        cols = slice(j * FF_CHUNK, (j + 1) * FF_CHUNK)
        gt = _dot(h, wi_ref[:, cols])
        up = _dot(h, wi_ref[:, D_FF + j * FF_CHUNK:D_FF + (j + 1) * FF_CHUNK])
        act_ref[:, cols] = (gt * jax.nn.sigmoid(gt) * up).astype(BF16)
    out_ref[...] = x + _dot(act_ref[...], wo_ref[...])


def _ffn(layer, x2d, ln, wi, wo, *, tm):
    m_rows = x2d.shape[0]
    assert m_rows % tm == 0 and D_FF % FF_CHUNK == 0
    row = pl.BlockSpec((tm, D_MODEL), lambda i: (i, 0))
    return pl.pallas_call(
        _ffn_kernel,
        grid=(m_rows // tm,),
        in_specs=[row, _const_spec(ln.shape), _layer_spec(wi, layer), _layer_spec(wo, layer)],
        out_specs=row,
        out_shape=jax.ShapeDtypeStruct((m_rows, D_MODEL), F32),
        scratch_shapes=[pltpu.VMEM((tm, D_FF), BF16)],
        compiler_params=_params(1),
        name="swiglu_ffn",
    )(x2d, ln, wi, wo)


def _rope_angles(pos):
    inv_freq = ROPE_THETA ** (-2.0 * jnp.arange(ROT_HALF, dtype=F32) / ROT_DIM)
    ang = pos[:, None] * inv_freq[None, :]
    return jnp.cos(ang), jnp.sin(ang)


def _rope_row_tables(pos):
    cos, sin = _rope_angles(pos)
    n = pos.shape[0]
    c = jnp.concatenate([cos, cos, jnp.ones((n, HEAD_DIM - ROT_DIM), F32)], axis=1)
    s = jnp.concatenate([-sin, sin, jnp.zeros((n, HEAD_DIM - ROT_DIM), F32)], axis=1)
    return jnp.tile(c, (1, N_KV)), jnp.tile(s, (1, N_KV))


def kernel(x_prompt, x_sample, cache_a, cache_b1, cache_b2, cache_b3, state_c, ln1, w_in, q_norm_a, k_norm_a,
           sink_a, q_norm_b, k_norm_b, conv_c, ws_d, bs_d, w_gate, b_gate, w_branch, w_o, ln2, w_ffn_in,
           w_ffn_out):
    bp, sp, _ = x_prompt.shape
    bs_n, ds, _ = x_sample.shape
    depth = ln1.shape[0]
    assert ds == 1 and sp >= CONV_W - 1
    caches = (cache_a, cache_b1, cache_b2, cache_b3)
    for c, (win, d) in zip(caches, MIXERS):
        assert c.shape[3] == win and win // d == BAND

    cos_p, sin_p = _rope_row_tables(jnp.arange(sp, dtype=F32))
    cos_s, sin_s = _rope_angles(PAST_LEN + jnp.arange(ds, dtype=F32))
    cos_col, sin_col = cos_s.reshape(ROT_HALF, 1), sin_s.reshape(ROT_HALF, 1)
    caches_t = tuple(jnp.transpose(c, (0, 1, 2, 4, 5, 3)) for c in caches)

    y_p = x_prompt.reshape(bp * sp, D_MODEL)
    y_s = x_sample.reshape(bs_n, D_MODEL)
    c_p, c_s, d_s = [], [], []
    kv_p = new_t = None
    assert SLOT_HEADS == tuple(kv * N_GROUP + g for g in range(N_GROUP) for kv in range(N_KV))
    w_att = w_in[:, :, :N_ATT].reshape(depth, D_MODEL, N_MIX, ATT_W)
    w_q = w_att[..., :QW].reshape(depth, D_MODEL, N_MIX, N_KV, N_GROUP, HEAD_DIM)
    w_q = jnp.swapaxes(w_q, 3, 4).reshape(depth, D_MODEL, N_MIX, QW)
    w_att = jnp.concatenate([w_q, w_att[..., QW:]], axis=-1).reshape(depth, D_MODEL, N_ATT)
    w_in_b = jnp.concatenate([w_att, w_in[:, :, N_ATT:]], axis=-1).astype(BF16)
    w_ab = w_branch[:, :2].reshape(depth, 2, N_KV, N_GROUP, HEAD_DIM, D_MODEL)
    w_ab = jnp.swapaxes(w_ab, 2, 3).reshape(depth, 2, QW, D_MODEL)
    wb_b = jnp.concatenate([w_ab, w_branch[:, 2:]], axis=1).astype(BF16)
    wg_b, wo_b = w_gate.astype(BF16), w_o.astype(BF16)
    wfi_b, wfo_b = w_ffn_in.astype(BF16), w_ffn_out.astype(BF16)
    for l in range(depth):
        ln1_l, ln2_l = ln1[l][None], ln2[l][None]
        gq_all = jnp.concatenate([q_norm_a[l][None], q_norm_b[l]], axis=0)
        gk_all = jnp.concatenate([k_norm_a[l][None], k_norm_b[l]], axis=0)
        bst = jnp.repeat(bs_d[l].T, HEAD_DIM, axis=1)

        h, zatt, oc, od, newc = _inproj_prompt(l, y_p, ln1_l, w_in_b, conv_c[l], ws_d[l], bst, seq=sp, tm=ROW_TILE)
        oa, ob, kv_p = _attn_prompt(l, depth, sink_a[l], zatt.reshape(N_ATT // LANES, bp, sp, LANES),
                                    jnp.tile(gq_all, (1, N_KV))[:, None], jnp.tile(gk_all, (1, N_KV))[:, None],
                                    cos_p, sin_p, kv_p)
        x1 = _merge(l, y_p, h, oa, ob, oc, od, wg_b, b_gate[l], wb_b, wo_b, tm=ROW_TILE)
        y_p = _ffn(l, x1, ln2_l, wfi_b, wfo_b, tm=ROW_TILE)
        c_p.append(newc)

        s0, s1 = state_c[l][:, 0], state_c[l][:, 1]
        w00 = jnp.repeat(ws_d[l][:, 0, 0], HEAD_DIM)[None]
        b0 = jnp.repeat(bs_d[l][:, 0], HEAD_DIM)[None]
        h, zatt, oc, od, zc, vd = _inproj_sample(l, y_s, ln1_l, w_in_b, conv_c[l], s0, s1, w00, b0)
        oat, obt, new_t = _decode(l, sink_a[l], zatt.T, gq_all.reshape(-1, 1), gk_all.reshape(-1, 1),
                                  cos_col, sin_col, caches_t, new_t, bb=DEC_BLOCK)
        to_slabs = lambda t: t.reshape(QW // LANES, LANES, bs_n).transpose(0, 2, 1)[None]
        x1 = _merge(l, y_s, h, to_slabs(oat), to_slabs(obt), oc, od, wg_b, b_gate[l], wb_b, wo_b, tm=bs_n)
        y_s = _ffn(l, x1, ln2_l, wfi_b, wfo_b, tm=bs_n)
        c_s.append(jnp.stack([s1, zc], axis=1))
        d_s.append(vd[:, None])

    def prompt_cache(m):
        t = kv_p[m]
        return jnp.transpose(t.reshape(depth, bp, 2, N_KV, HEAD_DIM, t.shape[-1]), (0, 1, 2, 5, 3, 4))

    sample_cache = [jnp.transpose(t, (0, 1, 2, 5, 3, 4)) for t in new_t]
    return (y_p.reshape(bp, sp, D_MODEL), y_s.reshape(bs_n, ds, D_MODEL),
            prompt_cache(0), sample_cache[0], prompt_cache(1), sample_cache[1],
            prompt_cache(2), sample_cache[2], prompt_cache(3), sample_cache[3],
            jnp.stack(c_p), jnp.stack(c_s), jnp.stack(d_s))
```

```python
import functools
import math

import jax
import jax.numpy as jnp
from jax import lax
from jax.experimental import pallas as pl
from jax.experimental.pallas import tpu as pltpu

F32 = jnp.float32
BF16 = jnp.bfloat16

D_MODEL = 1024
HEAD_DIM = 64
ROT_DIM = HEAD_DIM // 4
ROT_HALF = ROT_DIM // 2
ROPE_THETA = 500000.0
BRANCH_W = 384
N_HEADS = BRANCH_W // HEAD_DIM
N_KV = 2
N_GROUP = N_HEADS // N_KV
QW = N_HEADS * HEAD_DIM
KVW = N_KV * HEAD_DIM
ATT_W = QW + 2 * KVW
WIN_A = 128
DIL_GROUPS = ((128, 1), (512, 4), (2048, 16))
MIXERS = ((WIN_A, 1),) + DIL_GROUPS
N_MIX = len(MIXERS)
MIXER_ORDER = (0, 3, 2, 1)
BAND = 128
LANES = 128
SLOT_HEADS = (0, 3, 1, 4, 2, 5)
SLOT_OF_HEAD = tuple(SLOT_HEADS.index(h) for h in range(6))
CONV_W = 3
CHUNK = 128
N_SG = 6
N_BRANCH = 4
N_ATT = ATT_W * N_MIX
N_IN = N_ATT + 5 * BRANCH_W
D_FF = 2816
FF_CHUNK = 256
PAST_LEN = 16384
EPS = 1e-6
NEG = -1e30

VMEM_LIMIT_V7X = 56 * 1024 * 1024
ROW_TILE = 512
DEC_BLOCK = 2
BLOCK_UNROLL = 8
PROJ_CHUNK = 1280


def _params(n_axes, vmem=VMEM_LIMIT_V7X):
    return pltpu.CompilerParams(dimension_semantics=("arbitrary",) * n_axes, vmem_limit_bytes=vmem)


def _const_spec(shape):
    nd = len(shape)
    return pl.BlockSpec(shape, lambda *_: (0,) * nd)


def _layer_spec(stacked, layer):
    shape = stacked.shape[1:]
    return pl.BlockSpec((None,) + shape, lambda *_: (layer,) + (0,) * len(shape))


def _rms(x, g):
    return x * lax.rsqrt(jnp.mean(x * x, axis=-1, keepdims=True) + EPS) * g


def _dot(a, b):
    return jnp.dot(a, b, preferred_element_type=F32)


def _project_slabs(h, w_ref):
    n = w_ref.shape[1]
    pieces = []
    for lo in range(0, n, PROJ_CHUNK):
        hi = min(lo + PROJ_CHUNK, n)
        z = _dot(h, w_ref[:, lo:hi])
        pieces += [z[:, c * LANES:(c + 1) * LANES] for c in range((hi - lo) // LANES)]
    return pieces


def _project_branches(x_ref, ln_ref, wr_ref, h_ref):
    h = _rms(x_ref[...], ln_ref[...]).astype(BF16)
    h_ref[...] = h
    pieces = _project_slabs(h, wr_ref)
    per = BRANCH_W // LANES
    return h, [jnp.concatenate(pieces[j * per:(j + 1) * per], axis=1) for j in range(5)]


def _project_attention(h, wa_ref, zatt_ref, slabs):
    pieces = _project_slabs(h, wa_ref)
    if slabs:
        for c, piece in enumerate(pieces):
            zatt_ref[c] = piece
    else:
        zatt_ref[...] = jnp.concatenate(pieces, axis=1)


def _inproj_prompt_kernel(x_ref, ln_ref, wa_ref, wr_ref, conv_ref, ws_ref, bst_ref,
                          h_ref, zatt_ref, oc_ref, od_ref, newc_ref, carry_ref, *, tm, tiles_per_seq):
    i = pl.program_id(0)
    h, (gate_b, gate_c, x_c, u_d, v_d) = _project_branches(x_ref, ln_ref, wr_ref, h_ref)

    @pl.when(i % tiles_per_seq == 0)
    def _():
        carry_ref[...] = jnp.zeros_like(carry_ref)

    zc = gate_c * x_c
    carry = carry_ref[...]
    row = lax.broadcasted_iota(jnp.int32, zc.shape, 0)
    z1 = jnp.where(row == 0, carry[1:2], pltpu.roll(zc, 1, 0))
    z2 = jnp.where(row == 0, carry[0:1], jnp.where(row == 1, carry[1:2], pltpu.roll(zc, 2, 0)))
    conv = conv_ref[0:1] * z2 + conv_ref[1:2] * z1 + conv_ref[2:3] * zc
    oc_ref[...] = gate_b * conv
    tail = zc[tm - 2:tm]
    carry_ref[...] = tail
    newc_ref[...] = tail

    tri = (lax.broadcasted_iota(jnp.int32, (CHUNK, CHUNK), 0)
           >= lax.broadcasted_iota(jnp.int32, (CHUNK, CHUNK), 1))
    vb = v_d.astype(BF16)
    nchunk = tm // CHUNK
    mixed = []
    for g in range(N_SG):
        wm = jnp.where(tri, ws_ref[g], 0.0).astype(BF16)
        vg = jnp.concatenate([vb[c * CHUNK:(c + 1) * CHUNK, g * HEAD_DIM:(g + 1) * HEAD_DIM]
                              for c in range(nchunk)], axis=1)
        mixed.append(_dot(wm, vg))
    for c in range(nchunk):
        rows = slice(c * CHUNK, (c + 1) * CHUNK)
        sv = jnp.concatenate([mixed[g][:, c * HEAD_DIM:(c + 1) * HEAD_DIM] for g in range(N_SG)], axis=1)
        od_ref[rows, :] = u_d[rows] * (sv + bst_ref[...])

    _project_attention(h, wa_ref, zatt_ref, True)


def _inproj_sample_kernel(x_ref, ln_ref, wa_ref, wr_ref, conv_ref, s0_ref, s1_ref, w00_ref, b0_ref,
                          h_ref, zatt_ref, oc_ref, od_ref, zc_ref, vd_ref):
    h, (gate_b, gate_c, x_c, u_d, v_d) = _project_branches(x_ref, ln_ref, wr_ref, h_ref)
    _project_attention(h, wa_ref, zatt_ref, False)
    zc = gate_c * x_c
    conv = conv_ref[0:1] * s0_ref[...] + conv_ref[1:2] * s1_ref[...] + conv_ref[2:3] * zc
    oc_ref[...] = gate_b * conv
    zc_ref[...] = zc
    od_ref[...] = u_d * (v_d * w00_ref[...] + b0_ref[...])
    vd_ref[...] = v_d


def _inproj_prompt(layer, x2d, ln, w_att, w_rest, conv, ws, bst, *, seq, tm):
    m_rows = x2d.shape[0]
    assert m_rows % seq == 0 and seq % tm == 0 and tm % CHUNK == 0
    tps = seq // tm
    row = lambda w: pl.BlockSpec((tm, w), lambda i: (i, 0))
    return pl.pallas_call(
        functools.partial(_inproj_prompt_kernel, tm=tm, tiles_per_seq=tps),
        grid=(m_rows // tm,),
        in_specs=[row(D_MODEL), _const_spec((1, D_MODEL)), _layer_spec(w_att, layer), _layer_spec(w_rest, layer),
                  _const_spec((CONV_W, BRANCH_W)), _const_spec((N_SG, CHUNK, CHUNK)),
                  _const_spec((CHUNK, BRANCH_W))],
        out_specs=[row(D_MODEL), pl.BlockSpec((N_ATT // LANES, tm, LANES), lambda i: (0, i, 0)),
                   row(BRANCH_W), row(BRANCH_W),
                   pl.BlockSpec((None, CONV_W - 1, BRANCH_W), lambda i: (i // tps, 0, 0))],
        out_shape=[jax.ShapeDtypeStruct((m_rows, D_MODEL), BF16),
                   jax.ShapeDtypeStruct((N_ATT // LANES, m_rows, LANES), F32),
                   jax.ShapeDtypeStruct((m_rows, BRANCH_W), F32),
                   jax.ShapeDtypeStruct((m_rows, BRANCH_W), F32),
                   jax.ShapeDtypeStruct((m_rows // seq, CONV_W - 1, BRANCH_W), F32)],
        scratch_shapes=[pltpu.VMEM((CONV_W - 1, BRANCH_W), F32)],
        compiler_params=_params(1),
        name="inproj_prompt",
    )(x2d, ln, w_att, w_rest, conv, ws, bst)


def _inproj_sample(layer, x2d, ln, w_att, w_rest, conv, s0, s1, w00, b0):
    n = x2d.shape[0]
    full = lambda w: _const_spec((n, w))
    return pl.pallas_call(
        _inproj_sample_kernel,
        grid=(1,),
        in_specs=[full(D_MODEL), _const_spec((1, D_MODEL)), _layer_spec(w_att, layer), _layer_spec(w_rest, layer),
                  _const_spec((CONV_W, BRANCH_W)), full(BRANCH_W), full(BRANCH_W),
                  _const_spec((1, BRANCH_W)), _const_spec((1, BRANCH_W))],
        out_specs=[full(D_MODEL), full(N_ATT)] + [full(BRANCH_W)] * 4,
        out_shape=[jax.ShapeDtypeStruct((n, D_MODEL), BF16), jax.ShapeDtypeStruct((n, N_ATT), F32)]
                  + [jax.ShapeDtypeStruct((n, BRANCH_W), F32)] * 4,
        compiler_params=_params(1),
        name="inproj_sample",
    )(x2d, ln, w_att, w_rest, conv, s0, s1, w00, b0)


def _segment_ones(n):
    r = lax.broadcasted_iota(jnp.int32, (n, n), 0) // HEAD_DIM
    c = lax.broadcasted_iota(jnp.int32, (n, n), 1) // HEAD_DIM
    return (r == c).astype(BF16)


def _rotary_partner_matrix():
    j = lax.broadcasted_iota(jnp.int32, (LANES, LANES), 0)
    l = lax.broadcasted_iota(jnp.int32, (LANES, LANES), 1)
    e = l % HEAD_DIM
    first, second = e < ROT_HALF, (e >= ROT_HALF) & (e < ROT_DIM)
    return ((first & (j == l + ROT_HALF)) | (second & (j == l - ROT_HALF))).astype(BF16)


def _head_norm_rope(xs, gains, ones, partner, cos, sin):
    rows = xs[0].shape[0]
    ss = _dot(jnp.concatenate([(x * x).astype(BF16) for x in xs], axis=0), ones)
    xn = [x * lax.rsqrt(ss[c * rows:(c + 1) * rows] * (1.0 / HEAD_DIM) + EPS) * g
          for c, (x, g) in enumerate(zip(xs, gains))]
    xr = _dot(jnp.concatenate([t.astype(BF16) for t in xn], axis=0), partner)
    return [t * cos + xr[c * rows:(c + 1) * rows] * sin for c, t in enumerate(xn)]


def _attn_mixer(d, idx, seq, sink_ref, z_ref, gq_ref, gk_ref, cos_ref, sin_ref,
                oa_ref, cache_ref, qs_ref, kt_ref, vs_ref, kn_ref, obs_ref, lse_ref, bias_ref):
    with_sink, first_group = idx == 0, idx == MIXER_ORDER[1]
    nblk = seq // (BAND * d)
    has_prev = nblk > 1
    shift = nblk.bit_length() - 1
    ones = _segment_ones(LANES)
    partner = _rotary_partner_matrix()
    low = lax.broadcasted_iota(jnp.int32, (BAND, LANES), 1) < HEAD_DIM

    def natural_rows(it):
        if d == 1:
            return pl.ds(pl.multiple_of(it * BAND, BAND), BAND)
        return pl.ds((it >> shift) + (it & (nblk - 1)) * (BAND * d), BAND, stride=d)

    nb = seq // BAND
    nk = 2 * BAND if has_prev else BAND
    nq = QW // LANES
    ones_cols = jnp.ones((nk, LANES), BF16)

    def block_pos(b):
        return pl.ds(pl.multiple_of(b * BAND, BAND), BAND)

    def prev_block(b):
        return jnp.maximum(b - 1, 0)

    def prep_stage(it):
        rows = natural_rows(it)
        pos = block_pos(it)
        *qn, kn = _head_norm_rope([z_ref[c, rows, :] for c in range(nq + 1)],
                                  [gq_ref[...]] * nq + [gk_ref[...]], ones, partner,
                                  cos_ref[rows, :], sin_ref[rows, :])
        for c in range(nq):
            qc = (qn[c] * (1.0 / math.sqrt(HEAD_DIM))).astype(BF16)
            qs_ref[2 * c, pos, :] = jnp.where(low, qc, jnp.zeros_like(qc))
            qs_ref[2 * c + 1, pos, :] = jnp.where(low, jnp.zeros_like(qc), qc)
        kn_ref[rows, :] = kn
        kt_ref[it] = kn.T.astype(BF16)
        vs_ref[pos, :] = z_ref[nq + 1, rows, :].astype(BF16)

    def cache_stage():
        win = cache_ref.shape[2]
        for j in range(win // BAND):
            t0 = seq - win + j * BAND
            cache_ref[0, :, j * BAND:(j + 1) * BAND] = kn_ref[t0:t0 + BAND, :].T
            cache_ref[1, :, j * BAND:(j + 1) * BAND] = z_ref[nq + 1, t0:t0 + BAND, :].T

    def score_stage(b):
        q6 = jnp.concatenate([qs_ref[j, block_pos(b), :] for j in range(N_HEADS)], axis=0)
        if has_prev:
            first = (b & (nblk - 1)) == 0
            kt = jnp.concatenate([kt_ref[prev_block(b)], kt_ref[b]], axis=1)
            bias = bias_ref[jnp.where(first, 1, 0)]
        else:
            kt = kt_ref[b]
            bias = bias_ref[1, :, BAND:]
        return _dot(q6, kt) + jnp.concatenate([bias] * N_HEADS, axis=0)

    def softmax_stage(s):
        mx = jnp.max(s, axis=1, keepdims=True)
        p = jnp.exp(s - mx).astype(BF16)
        return p, [jnp.where(low, mx[2 * c * BAND:(2 * c + 1) * BAND], mx[(2 * c + 1) * BAND:(2 * c + 2) * BAND])
                   for c in range(nq)]

    def value_stage(b, p):
        v = vs_ref[block_pos(b), :]
        if has_prev:
            v = jnp.concatenate([vs_ref[block_pos(prev_block(b)), :], v], axis=0)
        v_aug = jnp.concatenate([v, ones_cols], axis=1)
        return _dot(p, v_aug)

    def output_stage(b, r2, mxs):
        pos = block_pos(b)
        rows = natural_rows(b)
        for c in range(nq):
            ra, rb = slice(2 * c * BAND, (2 * c + 1) * BAND), slice((2 * c + 1) * BAND, (2 * c + 2) * BAND)
            dn = jnp.where(low, r2[ra, LANES:], r2[rb, LANES:])
            mx = mxs[c]
            if with_sink:
                dn = dn + jnp.exp(jnp.where(low, sink_ref[SLOT_HEADS[2 * c]], sink_ref[SLOT_HEADS[2 * c + 1]]) - mx)
            o = jnp.where(low, r2[ra, :LANES], r2[rb, :LANES]) / dn
            if with_sink:
                oa_ref[c, pos, :] = o
                continue
            lse = mx + jnp.log(dn)
            if first_group:
                obs_ref[c, rows, :] = o
                lse_ref[c, rows, :] = lse
            else:
                lse_old = lse_ref[c, rows, :]
                top = jnp.maximum(lse_old, lse)
                w_old = jnp.exp(lse_old - top)
                w_new = jnp.exp(lse - top)
                tot = w_old + w_new
                obs_ref[c, rows, :] = (w_old * obs_ref[c, rows, :] + w_new * o) / tot
                lse_ref[c, rows, :] = top + jnp.log(tot)

    def prep_step(b, carry):
        prep_stage(jnp.asarray(b, jnp.int32))
        return carry

    def attn_step(b, carry):
        b = jnp.asarray(b, jnp.int32)
        p, mxs = softmax_stage(score_stage(b))
        output_stage(b, value_stage(b, p), mxs)
        return carry

    lax.fori_loop(0, nb, prep_step, 0, unroll=BLOCK_UNROLL)
    cache_stage()
    lax.fori_loop(0, nb, attn_step, 0, unroll=BLOCK_UNROLL)


def _attn_prompt_kernel(sink_ref, z_ref, gq_ref, gk_ref, cos_ref, sin_ref, *rest, seq, n_alias):
    (oa_ref, ob_ref, *cache_refs, qs_ref, kt_ref, vs_ref, kn_ref, obs_ref, lse_ref, bias_ref) = rest[n_alias:]
    m = pl.program_id(1)
    qi = lax.broadcasted_iota(jnp.int32, (BAND, 2 * BAND), 0)
    kj = lax.broadcasted_iota(jnp.int32, (BAND, 2 * BAND), 1)
    in_band = kj <= qi + BAND
    bias_ref[0] = jnp.where(in_band & (kj >= qi), 0.0, NEG)
    bias_ref[1] = jnp.where(in_band & (kj >= BAND), 0.0, NEG)

    for idx, (_, d) in enumerate(MIXERS):
        @pl.when(m == MIXER_ORDER.index(idx))
        def _(idx=idx, d=d):
            _attn_mixer(d, idx, seq, sink_ref, z_ref, gq_ref, gk_ref, cos_ref, sin_ref,
                        oa_ref, cache_refs[idx], qs_ref, kt_ref, vs_ref, kn_ref, obs_ref, lse_ref, bias_ref)

    @pl.when(m == N_MIX - 1)
    def _():
        ob_ref[...] = obs_ref[...]


def _attn_prompt(layer, depth, sink, z4, gq, gk, cos, sin, prev_caches):
    _, nb, seq, _ = z4.shape
    assert all(seq % (BAND * d) == 0 for _, d in MIXERS)
    per = ATT_W // LANES
    nq = QW // LANES
    tab = _const_spec((seq, LANES))
    mixer_at = lambda m: (N_MIX - m) % N_MIX
    assert all(MIXER_ORDER[m] == (N_MIX - m) % N_MIX for m in range(N_MIX))
    gain = pl.BlockSpec((None, 1, LANES), lambda b, m: (mixer_at(m), 0, 0))
    o_spec = pl.BlockSpec((None, nq, seq, LANES), lambda b, m: (b, 0, 0, 0))
    slabs = pltpu.VMEM((nq, seq, LANES), F32)
    wins = [min(win, seq) for win, _ in MIXERS]
    n_alias = 0 if prev_caches is None else len(prev_caches)
    n_in = 6
    outs = pl.pallas_call(
        functools.partial(_attn_prompt_kernel, seq=seq, n_alias=n_alias),
        grid=(nb, N_MIX),
        in_specs=[pl.BlockSpec(memory_space=pltpu.SMEM),
                  pl.BlockSpec((per, None, seq, LANES), lambda b, m: (mixer_at(m), b, 0, 0)),
                  gain, gain, tab, tab] + [pl.BlockSpec(memory_space=pl.ANY)] * n_alias,
        out_specs=[o_spec, o_spec]
                  + [pl.BlockSpec((None, None, 2, LANES, w), lambda b, m: (layer, b, 0, 0, 0)) for w in wins],
        out_shape=[jax.ShapeDtypeStruct((nb, nq, seq, LANES), F32),
                   jax.ShapeDtypeStruct((nb, nq, seq, LANES), F32)]
                  + [jax.ShapeDtypeStruct((depth, nb, 2, LANES, w), F32) for w in wins],
        scratch_shapes=[pltpu.VMEM((N_HEADS, seq, LANES), BF16),
                        pltpu.VMEM((seq // BAND, LANES, BAND), BF16),
                        pltpu.VMEM((seq, LANES), BF16),
                        pltpu.VMEM((seq, LANES), F32),
                        slabs, slabs,
                        pltpu.VMEM((2, BAND, 2 * BAND), F32)],
        input_output_aliases={n_in + j: 2 + j for j in range(n_alias)},
        compiler_params=_params(2),
        name="attn_prompt",
    )(sink, z4, gq, gk, cos, sin, *(prev_caches or ()))
    return outs[0], outs[1], tuple(outs[2:])


def _column_norm_rope(x, g, cos, sin):
    ss = jnp.sum(x * x, axis=0, keepdims=True)
    xn = x * lax.rsqrt(ss * (1.0 / HEAD_DIM) + EPS) * g
    x1, x2 = xn[0:ROT_HALF], xn[ROT_HALF:ROT_DIM]
    return jnp.concatenate([x1 * cos - x2 * sin, x2 * cos + x1 * sin, xn[ROT_DIM:]], axis=0)


def _decode_kernel(sink_ref, zt_ref, gq_ref, gk_ref, cos_ref, sin_ref,
                   ca_ref, c1_ref, c2_ref, c3_ref, *rest, bb, n_alias):
    rest = rest[n_alias:]
    oat_ref, obt_ref, na_ref, n1_ref, n2_ref, n3_ref, qn_ref, kn_ref, vn_ref = rest
    i = pl.program_id(0)
    nbatch = zt_ref.shape[1]
    cache_refs = (ca_ref, c1_ref, c2_ref, c3_ref)
    new_refs = (na_ref, n1_ref, n2_ref, n3_ref)

    @pl.when(i == 0)
    def _():
        cos, sin = cos_ref[...], sin_ref[...]
        for m in range(N_MIX):
            for h in range(N_HEADS):
                r = m * ATT_W + h * HEAD_DIM
                o = m * QW + h * HEAD_DIM
                qn_ref[o:o + HEAD_DIM, :] = _column_norm_rope(
                    zt_ref[r:r + HEAD_DIM, :], gq_ref[m * HEAD_DIM:(m + 1) * HEAD_DIM, :], cos, sin
                ) * (1.0 / math.sqrt(HEAD_DIM))
            for kvh in range(N_KV):
                r = m * ATT_W + QW + kvh * HEAD_DIM
                o = m * KVW + kvh * HEAD_DIM
                kn_ref[o:o + HEAD_DIM, :] = _column_norm_rope(
                    zt_ref[r:r + HEAD_DIM, :], gk_ref[m * HEAD_DIM:(m + 1) * HEAD_DIM, :], cos, sin)
            vn_ref[m * KVW:(m + 1) * KVW, :] = zt_ref[m * ATT_W + QW + KVW:(m + 1) * ATT_W, :]
        oat_ref[...] = jnp.zeros_like(oat_ref)
        obt_ref[...] = jnp.zeros_like(obt_ref)

    lane_b = lax.broadcasted_iota(jnp.int32, (1, nbatch), 1)

    def per_head(x):
        return jnp.concatenate([jnp.broadcast_to(x[h:h + 1], (HEAD_DIM, 1)) for h in range(N_HEADS)], axis=0)

    def slot_rows(col):
        return jnp.concatenate([col[h * HEAD_DIM:(h + 1) * HEAD_DIM] for h in SLOT_HEADS], axis=0)

    def per_b(b):
        sel = lane_b == (i * bb + b)

        def column(ref):
            return jnp.sum(jnp.where(sel, ref[...], 0.0), axis=1, keepdims=True)

        qcol, kcol, vcol = column(qn_ref), column(kn_ref), column(vn_ref)
        run = None
        for m, (win, d) in enumerate(MIXERS):
            cref, nref = cache_refs[m], new_refs[m]
            last = lax.broadcasted_iota(jnp.int32, (HEAD_DIM, win), 1) == win - 1
            s_rows, new_rows, kv_new, vts = [], [], [], []
            for kvh in range(N_KV):
                kt, vt = cref[b, 0, kvh], cref[b, 1, kvh]
                r = m * KVW + kvh * HEAD_DIM
                kn, vn = kcol[r:r + HEAD_DIM], vcol[r:r + HEAD_DIM]
                nref[b, 0, kvh] = jnp.where(last, kn, pltpu.roll(kt, win - 1, 1))
                nref[b, 1, kvh] = jnp.where(last, vn, pltpu.roll(vt, win - 1, 1))
                for g in range(N_GROUP):
                    r = m * QW + SLOT_OF_HEAD[N_GROUP * kvh + g] * HEAD_DIM
                    qc = qcol[r:r + HEAD_DIM]
                    s_rows.append(jnp.sum(qc * kt, axis=0, keepdims=True))
                    new_rows.append(jnp.sum(qc * kn, axis=0, keepdims=True))
                vts.append(vt)
                kv_new.append(vn)
            s, s_new = jnp.concatenate(s_rows, axis=0), jnp.concatenate(new_rows, axis=0)
            if d > 1:
                s = jnp.where((lax.broadcasted_iota(jnp.int32, (1, win), 1) % d) == 0, s, NEG)
            mx = jnp.maximum(jnp.max(s, axis=1, keepdims=True), s_new)
            if m == 0:
                sink = jnp.concatenate([jnp.full((1, 1), sink_ref[h], F32) for h in range(N_HEADS)], axis=0)
                mx = jnp.maximum(mx, sink)
            p, p_new = jnp.exp(s - mx), jnp.exp(s_new - mx)
            l = jnp.sum(p, axis=1, keepdims=True) + p_new
            if m == 0:
                l = l + jnp.exp(sink - mx)
            folded = []
            for h in range(N_HEADS):
                pv = p[h:h + 1] * vts[h // N_GROUP]
                f = pv[:, 0:LANES]
                for j in range(1, win // LANES):
                    f = f + pv[:, j * LANES:(j + 1) * LANES]
                folded.append(f)
            acc = jnp.sum(jnp.concatenate(folded, axis=0), axis=1, keepdims=True)
            acc = acc + per_head(p_new) * jnp.concatenate([kv_new[h // N_GROUP] for h in range(N_HEADS)], axis=0)
            if m == 0:
                oat_ref[...] = jnp.where(sel, slot_rows(acc / per_head(l)), oat_ref[...])
            elif run is None:
                run = (mx, l, acc)
            else:
                mx0, l0, acc0 = run
                top = jnp.maximum(mx0, mx)
                a0, a1 = jnp.exp(mx0 - top), jnp.exp(mx - top)
                run = (top, a0 * l0 + a1 * l, per_head(a0) * acc0 + per_head(a1) * acc)
        _, l, acc = run
        obt_ref[...] = jnp.where(sel, slot_rows(acc / per_head(l)), obt_ref[...])

    for b in range(bb):
        per_b(b)


def _decode(layer, sink, zt, gq_col, gk_col, cos_col, sin_col, caches_t, prev_out, *, bb):
    nbatch = zt.shape[1]
    assert nbatch % bb == 0
    cache_specs = [pl.BlockSpec((None, bb) + c.shape[2:], lambda i: (layer, i, 0, 0, 0, 0)) for c in caches_t]
    n_alias = 0 if prev_out is None else len(prev_out)
    alias_specs = [pl.BlockSpec(memory_space=pl.ANY)] * n_alias
    n_in = 6 + len(caches_t)
    outs = pl.pallas_call(
        functools.partial(_decode_kernel, bb=bb, n_alias=n_alias),
        grid=(nbatch // bb,),
        in_specs=[pl.BlockSpec(memory_space=pltpu.SMEM), _const_spec(zt.shape),
                  _const_spec(gq_col.shape), _const_spec(gk_col.shape),
                  _const_spec(cos_col.shape), _const_spec(sin_col.shape)] + cache_specs + alias_specs,
        out_specs=[_const_spec((QW, nbatch)), _const_spec((QW, nbatch))] + cache_specs,
        out_shape=[jax.ShapeDtypeStruct((QW, nbatch), F32), jax.ShapeDtypeStruct((QW, nbatch), F32)]
                  + [jax.ShapeDtypeStruct(c.shape, F32) for c in caches_t],
        scratch_shapes=[pltpu.VMEM((N_MIX * QW, nbatch), F32), pltpu.VMEM((N_MIX * KVW, nbatch), F32),
                        pltpu.VMEM((N_MIX * KVW, nbatch), F32)],
        input_output_aliases={n_in + j: 2 + j for j in range(n_alias)},
        compiler_params=_params(1),
        name="decode_attn_cache",
    )(sink, zt, gq_col, gk_col, cos_col, sin_col, *caches_t, *(prev_out or ()))
    return outs[0], outs[1], tuple(outs[2:])


def _merge_kernel(x_ref, h_ref, oa_ref, ob_ref, oc_ref, od_ref, wg_ref, bg_ref, wb_ref, wo_ref, out_ref):
    h = h_ref[...]
    slabs = lambda ref: jnp.concatenate([ref[c] for c in range(ref.shape[0])], axis=1)
    branches = (slabs(oa_ref), slabs(ob_ref), oc_ref[...], od_ref[...])
    merged = None
    for n, br in enumerate(branches):
        gate = jax.nn.sigmoid(_dot(h, wg_ref[n]) + bg_ref[n:n + 1])
        t = gate * _dot(br.astype(BF16), wb_ref[n])
        merged = t if merged is None else merged + t
    out_ref[...] = x_ref[...] + _dot(merged.astype(BF16), wo_ref[...])


def _merge(layer, x2d, h, oa, ob, oc, od, wg, bg, wb, wo, *, tm):
    m_rows = x2d.shape[0]
    seq = oa.shape[2]
    assert m_rows % tm == 0 and seq % tm == 0
    tps = seq // tm
    row = lambda w: pl.BlockSpec((tm, w), lambda i: (i, 0))
    slab = pl.BlockSpec((None, QW // LANES, tm, LANES), lambda i: (i // tps, 0, i % tps, 0))
    return pl.pallas_call(
        _merge_kernel,
        grid=(m_rows // tm,),
        in_specs=[row(D_MODEL), row(D_MODEL), slab, slab, row(BRANCH_W), row(BRANCH_W)]
                 + [_layer_spec(wg, layer), _const_spec(bg.shape), _layer_spec(wb, layer), _layer_spec(wo, layer)],
        out_specs=row(D_MODEL),
        out_shape=jax.ShapeDtypeStruct((m_rows, D_MODEL), F32),
        compiler_params=_params(1),
        name="branch_merge",
    )(x2d, h, oa, ob, oc, od, wg, bg, wb, wo)


def _ffn_kernel(x_ref, ln_ref, wi_ref, wo_ref, out_ref, act_ref):
    x = x_ref[...]
    h = _rms(x, ln_ref[...]).astype(BF16)
    for j in range(D_FF // FF_CHUNK):
        cols = slice(j * FF_CHUNK, (j + 1) * FF_CHUNK)
        gt = _dot(h, wi_ref[:, cols])
        up = _dot(h, wi_ref[:, D_FF + j * FF_CHUNK:D_FF + (j + 1) * FF_CHUNK])
        act_ref[:, cols] = (gt * jax.nn.sigmoid(gt) * up).astype(BF16)
    out_ref[...] = x + _dot(act_ref[...], wo_ref[...])


def _ffn(layer, x2d, ln, wi, wo, *, tm):
    m_rows = x2d.shape[0]
    assert m_rows % tm == 0 and D_FF % FF_CHUNK == 0
    row = pl.BlockSpec((tm, D_MODEL), lambda i: (i, 0))
    return pl.pallas_call(
        _ffn_kernel,
        grid=(m_rows // tm,),
        in_specs=[row, _const_spec(ln.shape), _layer_spec(wi, layer), _layer_spec(wo, layer)],
        out_specs=row,
        out_shape=jax.ShapeDtypeStruct((m_rows, D_MODEL), F32),
        scratch_shapes=[pltpu.VMEM((tm, D_FF), BF16)],
        compiler_params=_params(1),
        name="swiglu_ffn",
    )(x2d, ln, wi, wo)


def _rope_angles(pos):
    inv_freq = ROPE_THETA ** (-2.0 * jnp.arange(ROT_HALF, dtype=F32) / ROT_DIM)
    ang = pos[:, None] * inv_freq[None, :]
    return jnp.cos(ang), jnp.sin(ang)


def _rope_row_tables(pos):
    cos, sin = _rope_angles(pos)
    n = pos.shape[0]
    c = jnp.concatenate([cos, cos, jnp.ones((n, HEAD_DIM - ROT_DIM), F32)], axis=1)
    s = jnp.concatenate([-sin, sin, jnp.zeros((n, HEAD_DIM - ROT_DIM), F32)], axis=1)
    return jnp.tile(c, (1, N_KV)), jnp.tile(s, (1, N_KV))


def kernel(x_prompt, x_sample, cache_a, cache_b1, cache_b2, cache_b3, state_c, ln1, w_in, q_norm_a, k_norm_a,
           sink_a, q_norm_b, k_norm_b, conv_c, ws_d, bs_d, w_gate, b_gate, w_branch, w_o, ln2, w_ffn_in,
           w_ffn_out):
    bp, sp, _ = x_prompt.shape
    bs_n, ds, _ = x_sample.shape
    depth = ln1.shape[0]
    assert ds == 1 and sp >= CONV_W - 1
    caches = (cache_a, cache_b1, cache_b2, cache_b3)
    for c, (win, d) in zip(caches, MIXERS):
        assert c.shape[3] == win and win // d == BAND

    cos_p, sin_p = _rope_row_tables(jnp.arange(sp, dtype=F32))
    cos_s, sin_s = _rope_angles(PAST_LEN + jnp.arange(ds, dtype=F32))
    cos_col, sin_col = cos_s.reshape(ROT_HALF, 1), sin_s.reshape(ROT_HALF, 1)
    caches_t = tuple(jnp.transpose(c, (0, 1, 2, 4, 5, 3)) for c in caches)

    y_p = x_prompt.reshape(bp * sp, D_MODEL)
    y_s = x_sample.reshape(bs_n, D_MODEL)
    c_p, c_s, d_s = [], [], []
    kv_p = new_t = None
    assert SLOT_HEADS == tuple(kv * N_GROUP + g for g in range(N_GROUP) for kv in range(N_KV))
    w_att = w_in[:, :, :N_ATT].reshape(depth, D_MODEL, N_MIX, ATT_W)
    w_q = w_att[..., :QW].reshape(depth, D_MODEL, N_MIX, N_KV, N_GROUP, HEAD_DIM)
    w_q = jnp.swapaxes(w_q, 3, 4).reshape(depth, D_MODEL, N_MIX, QW)
    w_att_b = jnp.concatenate([w_q, w_att[..., QW:]], axis=-1).reshape(depth, D_MODEL, N_ATT).astype(BF16)
    w_rest_b = w_in[:, :, N_ATT:].astype(BF16)
    w_ab = w_branch[:, :2].reshape(depth, 2, N_KV, N_GROUP, HEAD_DIM, D_MODEL)
    w_ab = jnp.swapaxes(w_ab, 2, 3).reshape(depth, 2, QW, D_MODEL)
    wb_b = jnp.concatenate([w_ab, w_branch[:, 2:]], axis=1).astype(BF16)
    wg_b, wo_b = w_gate.astype(BF16), w_o.astype(BF16)
    wfi_b, wfo_b = w_ffn_in.astype(BF16), w_ffn_out.astype(BF16)
    for l in range(depth):
        ln1_l, ln2_l = ln1[l][None], ln2[l][None]
        gq_all = jnp.concatenate([q_norm_a[l][None], q_norm_b[l]], axis=0)
        gk_all = jnp.concatenate([k_norm_a[l][None], k_norm_b[l]], axis=0)
        bst = jnp.repeat(bs_d[l].T, HEAD_DIM, axis=1)

        h, zatt, oc, od, newc = _inproj_prompt(l, y_p, ln1_l, w_att_b, w_rest_b, conv_c[l], ws_d[l], bst,
                                               seq=sp, tm=ROW_TILE)
        oa, ob, kv_p = _attn_prompt(l, depth, sink_a[l], zatt.reshape(N_ATT // LANES, bp, sp, LANES),
                                    jnp.tile(gq_all, (1, N_KV))[:, None], jnp.tile(gk_all, (1, N_KV))[:, None],
                                    cos_p, sin_p, kv_p)
        x1 = _merge(l, y_p, h, oa, ob, oc, od, wg_b, b_gate[l], wb_b, wo_b, tm=ROW_TILE)
        y_p = _ffn(l, x1, ln2_l, wfi_b, wfo_b, tm=ROW_TILE)
        c_p.append(newc)

        s0, s1 = state_c[l][:, 0], state_c[l][:, 1]
        w00 = jnp.repeat(ws_d[l][:, 0, 0], HEAD_DIM)[None]
        b0 = jnp.repeat(bs_d[l][:, 0], HEAD_DIM)[None]
        h, zatt, oc, od, zc, vd = _inproj_sample(l, y_s, ln1_l, w_att_b, w_rest_b, conv_c[l], s0, s1, w00, b0)
        oat, obt, new_t = _decode(l, sink_a[l], zatt.T, gq_all.reshape(-1, 1), gk_all.reshape(-1, 1),
                                  cos_col, sin_col, caches_t, new_t, bb=DEC_BLOCK)
        to_slabs = lambda t: t.reshape(QW // LANES, LANES, bs_n).transpose(0, 2, 1)[None]
        x1 = _merge(l, y_s, h, to_slabs(oat), to_slabs(obt), oc, od, wg_b, b_gate[l], wb_b, wo_b, tm=bs_n)
        y_s = _ffn(l, x1, ln2_l, wfi_b, wfo_b, tm=bs_n)
        c_s.append(jnp.stack([s1, zc], axis=1))
        d_s.append(vd[:, None])

    def prompt_cache(m):
        t = kv_p[m]
        return jnp.transpose(t.reshape(depth, bp, 2, N_KV, HEAD_DIM, t.shape[-1]), (0, 1, 2, 5, 3, 4))

    sample_cache = [jnp.transpose(t, (0, 1, 2, 5, 3, 4)) for t in new_t]
    return (y_p.reshape(bp, sp, D_MODEL), y_s.reshape(bs_n, ds, D_MODEL),
            prompt_cache(0), sample_cache[0], prompt_cache(1), sample_cache[1],
            prompt_cache(2), sample_cache[2], prompt_cache(3), sample_cache[3],
            jnp.stack(c_p), jnp.stack(c_s), jnp.stack(d_s))
```

```python
import functools
import math

import jax
import jax.numpy as jnp
from jax import lax
from jax.experimental import pallas as pl
from jax.experimental.pallas import tpu as pltpu

F32 = jnp.float32
BF16 = jnp.bfloat16

D_MODEL = 1024
HEAD_DIM = 64
ROT_DIM = HEAD_DIM // 4
ROT_HALF = ROT_DIM // 2
ROPE_THETA = 500000.0
BRANCH_W = 384
N_HEADS = BRANCH_W // HEAD_DIM
N_KV = 2
N_GROUP = N_HEADS // N_KV
QW = N_HEADS * HEAD_DIM
KVW = N_KV * HEAD_DIM
ATT_W = QW + 2 * KVW
WIN_A = 128
DIL_GROUPS = ((128, 1), (512, 4), (2048, 16))
MIXERS = ((WIN_A, 1),) + DIL_GROUPS
N_MIX = len(MIXERS)
MIXER_ORDER = (0, 3, 2, 1)
BAND = 128
LANES = 128
SLOT_HEADS = (0, 3, 1, 4, 2, 5)
SLOT_OF_HEAD = tuple(SLOT_HEADS.index(h) for h in range(6))
CONV_W = 3
CHUNK = 128
N_SG = 6
N_BRANCH = 4
N_ATT = ATT_W * N_MIX
N_IN = N_ATT + 5 * BRANCH_W
D_FF = 2816
FF_CHUNK = 256
PAST_LEN = 16384
EPS = 1e-6
NEG = -1e30

VMEM_LIMIT_V7X = 56 * 1024 * 1024
ROW_TILE = 512
DEC_BLOCK = 2
BLOCK_UNROLL = 8
PROJ_CHUNK = 1280


def _params(n_axes, vmem=VMEM_LIMIT_V7X):
    return pltpu.CompilerParams(dimension_semantics=("arbitrary",) * n_axes, vmem_limit_bytes=vmem)


def _const_spec(shape):
    nd = len(shape)
    return pl.BlockSpec(shape, lambda *_: (0,) * nd)


def _layer_spec(stacked, layer):
    shape = stacked.shape[1:]
    return pl.BlockSpec((None,) + shape, lambda *_: (layer,) + (0,) * len(shape))


def _rms(x, g):
    return x * lax.rsqrt(jnp.mean(x * x, axis=-1, keepdims=True) + EPS) * g


def _dot(a, b):
    return jnp.dot(a, b, preferred_element_type=F32)


def _project_slabs(h, w_ref):
    n = w_ref.shape[1]
    pieces = []
    for lo in range(0, n, PROJ_CHUNK):
        hi = min(lo + PROJ_CHUNK, n)
        z = _dot(h, w_ref[:, lo:hi])
        pieces += [z[:, c * LANES:(c + 1) * LANES] for c in range((hi - lo) // LANES)]
    return pieces


def _project_branches(x_ref, ln_ref, wr_ref, h_ref):
    h = _rms(x_ref[...], ln_ref[...]).astype(BF16)
    h_ref[...] = h
    pieces = _project_slabs(h, wr_ref)
    per = BRANCH_W // LANES
    return h, [jnp.concatenate(pieces[j * per:(j + 1) * per], axis=1) for j in range(5)]


def _project_attention(h, wa_ref, zatt_ref, slabs):
    pieces = _project_slabs(h, wa_ref)
    if slabs:
        for c, piece in enumerate(pieces):
            zatt_ref[c] = piece
    else:
        zatt_ref[...] = jnp.concatenate(pieces, axis=1)


def _inproj_prompt_kernel(x_ref, ln_ref, wa_ref, wr_ref, conv_ref, ws_ref, bst_ref,
                          h_ref, zatt_ref, oc_ref, od_ref, newc_ref, carry_ref, *, tm, tiles_per_seq):
    i = pl.program_id(0)
    h, (gate_b, gate_c, x_c, u_d, v_d) = _project_branches(x_ref, ln_ref, wr_ref, h_ref)

    @pl.when(i % tiles_per_seq == 0)
    def _():
        carry_ref[...] = jnp.zeros_like(carry_ref)

    zc = gate_c * x_c
    carry = carry_ref[...]
    row = lax.broadcasted_iota(jnp.int32, zc.shape, 0)
    z1 = jnp.where(row == 0, carry[1:2], pltpu.roll(zc, 1, 0))
    z2 = jnp.where(row == 0, carry[0:1], jnp.where(row == 1, carry[1:2], pltpu.roll(zc, 2, 0)))
    conv = conv_ref[0:1] * z2 + conv_ref[1:2] * z1 + conv_ref[2:3] * zc
    oc_ref[...] = gate_b * conv
    tail = zc[tm - 2:tm]
    carry_ref[...] = tail
    newc_ref[...] = tail

    tri = (lax.broadcasted_iota(jnp.int32, (CHUNK, CHUNK), 0)
           >= lax.broadcasted_iota(jnp.int32, (CHUNK, CHUNK), 1))
    vb = v_d.astype(BF16)
    nchunk = tm // CHUNK
    mixed = []
    for g in range(N_SG):
        wm = jnp.where(tri, ws_ref[g], 0.0).astype(BF16)
        vg = jnp.concatenate([vb[c * CHUNK:(c + 1) * CHUNK, g * HEAD_DIM:(g + 1) * HEAD_DIM]
                              for c in range(nchunk)], axis=1)
        mixed.append(_dot(wm, vg))
    for c in range(nchunk):
        rows = slice(c * CHUNK, (c + 1) * CHUNK)
        sv = jnp.concatenate([mixed[g][:, c * HEAD_DIM:(c + 1) * HEAD_DIM] for g in range(N_SG)], axis=1)
        od_ref[rows, :] = u_d[rows] * (sv + bst_ref[...])

    _project_attention(h, wa_ref, zatt_ref, True)


def _inproj_sample_kernel(x_ref, ln_ref, wa_ref, wr_ref, conv_ref, s0_ref, s1_ref, w00_ref, b0_ref,
                          h_ref, zatt_ref, oc_ref, od_ref, zc_ref, vd_ref):
    h, (gate_b, gate_c, x_c, u_d, v_d) = _project_branches(x_ref, ln_ref, wr_ref, h_ref)
    _project_attention(h, wa_ref, zatt_ref, False)
    zc = gate_c * x_c
    conv = conv_ref[0:1] * s0_ref[...] + conv_ref[1:2] * s1_ref[...] + conv_ref[2:3] * zc
    oc_ref[...] = gate_b * conv
    zc_ref[...] = zc
    od_ref[...] = u_d * (v_d * w00_ref[...] + b0_ref[...])
    vd_ref[...] = v_d


def _inproj_prompt(layer, x2d, ln, w_att, w_rest, conv, ws, bst, *, seq, tm):
    m_rows = x2d.shape[0]
    assert m_rows % seq == 0 and seq % tm == 0 and tm % CHUNK == 0
    tps = seq // tm
    row = lambda w: pl.BlockSpec((tm, w), lambda i: (i, 0))
    return pl.pallas_call(
        functools.partial(_inproj_prompt_kernel, tm=tm, tiles_per_seq=tps),
        grid=(m_rows // tm,),
        in_specs=[row(D_MODEL), _const_spec((1, D_MODEL)), _layer_spec(w_att, layer), _layer_spec(w_rest, layer),
                  _const_spec((CONV_W, BRANCH_W)), _const_spec((N_SG, CHUNK, CHUNK)),
                  _const_spec((CHUNK, BRANCH_W))],
        out_specs=[row(D_MODEL), pl.BlockSpec((N_ATT // LANES, tm, LANES), lambda i: (0, i, 0)),
                   row(BRANCH_W), row(BRANCH_W),
                   pl.BlockSpec((None, CONV_W - 1, BRANCH_W), lambda i: (i // tps, 0, 0))],
        out_shape=[jax.ShapeDtypeStruct((m_rows, D_MODEL), BF16),
                   jax.ShapeDtypeStruct((N_ATT // LANES, m_rows, LANES), F32),
                   jax.ShapeDtypeStruct((m_rows, BRANCH_W), F32),
                   jax.ShapeDtypeStruct((m_rows, BRANCH_W), F32),
                   jax.ShapeDtypeStruct((m_rows // seq, CONV_W - 1, BRANCH_W), F32)],
        scratch_shapes=[pltpu.VMEM((CONV_W - 1, BRANCH_W), F32)],
        compiler_params=_params(1),
        name="inproj_prompt",
    )(x2d, ln, w_att, w_rest, conv, ws, bst)


def _inproj_sample(layer, x2d, ln, w_att, w_rest, conv, s0, s1, w00, b0):
    n = x2d.shape[0]
    full = lambda w: _const_spec((n, w))
    return pl.pallas_call(
        _inproj_sample_kernel,
        grid=(1,),
        in_specs=[full(D_MODEL), _const_spec((1, D_MODEL)), _layer_spec(w_att, layer), _layer_spec(w_rest, layer),
                  _const_spec((CONV_W, BRANCH_W)), full(BRANCH_W), full(BRANCH_W),
                  _const_spec((1, BRANCH_W)), _const_spec((1, BRANCH_W))],
        out_specs=[full(D_MODEL), full(N_ATT)] + [full(BRANCH_W)] * 4,
        out_shape=[jax.ShapeDtypeStruct((n, D_MODEL), BF16), jax.ShapeDtypeStruct((n, N_ATT), F32)]
                  + [jax.ShapeDtypeStruct((n, BRANCH_W), F32)] * 4,
        compiler_params=_params(1),
        name="inproj_sample",
    )(x2d, ln, w_att, w_rest, conv, s0, s1, w00, b0)


def _segment_ones(n):
    r = lax.broadcasted_iota(jnp.int32, (n, n), 0) // HEAD_DIM
    c = lax.broadcasted_iota(jnp.int32, (n, n), 1) // HEAD_DIM
    return (r == c).astype(BF16)


def _rotary_partner_matrix():
    j = lax.broadcasted_iota(jnp.int32, (LANES, LANES), 0)
    l = lax.broadcasted_iota(jnp.int32, (LANES, LANES), 1)
    e = l % HEAD_DIM
    first, second = e < ROT_HALF, (e >= ROT_HALF) & (e < ROT_DIM)
    return ((first & (j == l + ROT_HALF)) | (second & (j == l - ROT_HALF))).astype(BF16)


def _head_norm_rope(xs, gains, ones, partner, cos, sin):
    rows = xs[0].shape[0]
    ss = _dot(jnp.concatenate([(x * x).astype(BF16) for x in xs], axis=0), ones)
    xn = [x * lax.rsqrt(ss[c * rows:(c + 1) * rows] * (1.0 / HEAD_DIM) + EPS) * g
          for c, (x, g) in enumerate(zip(xs, gains))]
    xr = _dot(jnp.concatenate([t.astype(BF16) for t in xn], axis=0), partner)
    return [t * cos + xr[c * rows:(c + 1) * rows] * sin for c, t in enumerate(xn)]


def _attn_mixer(d, idx, seq, sink_ref, z_ref, gq_ref, gk_ref, cos_ref, sin_ref,
                oa_ref, cache_ref, qs_ref, kt_ref, vs_ref, kn_ref, obs_ref, lse_ref, bias_ref):
    with_sink, first_group = idx == 0, idx == MIXER_ORDER[1]
    nblk = seq // (BAND * d)
    has_prev = nblk > 1
    shift = nblk.bit_length() - 1
    ones = _segment_ones(LANES)
    partner = _rotary_partner_matrix()
    low = lax.broadcasted_iota(jnp.int32, (BAND, LANES), 1) < HEAD_DIM

    def natural_rows(it):
        if d == 1:
            return pl.ds(pl.multiple_of(it * BAND, BAND), BAND)
        return pl.ds((it >> shift) + (it & (nblk - 1)) * (BAND * d), BAND, stride=d)

    nb = seq // BAND
    nk = 2 * BAND if has_prev else BAND
    nq = QW // LANES
    ones_cols = jnp.ones((nk, LANES), BF16)

    def block_pos(b):
        return pl.ds(pl.multiple_of(b * BAND, BAND), BAND)

    def prev_block(b):
        return jnp.maximum(b - 1, 0)

    def prep_stage(it):
        rows = natural_rows(it)
        pos = block_pos(it)
        *qn, kn = _head_norm_rope([z_ref[c, rows, :] for c in range(nq + 1)],
                                  [gq_ref[...]] * nq + [gk_ref[...]], ones, partner,
                                  cos_ref[rows, :], sin_ref[rows, :])
        for c in range(nq):
            qc = (qn[c] * (1.0 / math.sqrt(HEAD_DIM))).astype(BF16)
            qs_ref[2 * c, pos, :] = jnp.where(low, qc, jnp.zeros_like(qc))
            qs_ref[2 * c + 1, pos, :] = jnp.where(low, jnp.zeros_like(qc), qc)
        kn_ref[rows, :] = kn
        kt_ref[it] = kn.T.astype(BF16)
        vs_ref[pos, :] = z_ref[nq + 1, rows, :].astype(BF16)

    def cache_stage():
        win = cache_ref.shape[2]
        for j in range(win // BAND):
            t0 = seq - win + j * BAND
            cache_ref[0, :, j * BAND:(j + 1) * BAND] = kn_ref[t0:t0 + BAND, :].T
            cache_ref[1, :, j * BAND:(j + 1) * BAND] = z_ref[nq + 1, t0:t0 + BAND, :].T

    def score_stage(b):
        q6 = jnp.concatenate([qs_ref[j, block_pos(b), :] for j in range(N_HEADS)], axis=0)
        if has_prev:
            first = (b & (nblk - 1)) == 0
            kt = jnp.concatenate([kt_ref[prev_block(b)], kt_ref[b]], axis=1)
            bias = bias_ref[jnp.where(first, 1, 0)]
        else:
            kt = kt_ref[b]
            bias = bias_ref[1, :, BAND:]
        return _dot(q6, kt) + jnp.concatenate([bias] * N_HEADS, axis=0)

    def softmax_stage(s):
        mx = jnp.max(s, axis=1, keepdims=True)
        p = jnp.exp(s - mx).astype(BF16)
        return p, [jnp.where(low, mx[2 * c * BAND:(2 * c + 1) * BAND], mx[(2 * c + 1) * BAND:(2 * c + 2) * BAND])
                   for c in range(nq)]

    def value_stage(b, p):
        v = vs_ref[block_pos(b), :]
        if has_prev:
            v = jnp.concatenate([vs_ref[block_pos(prev_block(b)), :], v], axis=0)
        v_aug = jnp.concatenate([v, ones_cols], axis=1)
        return _dot(p, v_aug)

    def output_stage(b, r2, mxs):
        pos = block_pos(b)
        rows = natural_rows(b)
        for c in range(nq):
            ra, rb = slice(2 * c * BAND, (2 * c + 1) * BAND), slice((2 * c + 1) * BAND, (2 * c + 2) * BAND)
            dn = jnp.where(low, r2[ra, LANES:], r2[rb, LANES:])
            mx = mxs[c]
            if with_sink:
                dn = dn + jnp.exp(jnp.where(low, sink_ref[SLOT_HEADS[2 * c]], sink_ref[SLOT_HEADS[2 * c + 1]]) - mx)
            o = jnp.where(low, r2[ra, :LANES], r2[rb, :LANES]) / dn
            if with_sink:
                oa_ref[c, pos, :] = o
                continue
            lse = mx + jnp.log(dn)
            if first_group:
                obs_ref[c, rows, :] = o
                lse_ref[c, rows, :] = lse
            else:
                lse_old = lse_ref[c, rows, :]
                top = jnp.maximum(lse_old, lse)
                w_old = jnp.exp(lse_old - top)
                w_new = jnp.exp(lse - top)
                tot = w_old + w_new
                obs_ref[c, rows, :] = (w_old * obs_ref[c, rows, :] + w_new * o) / tot
                lse_ref[c, rows, :] = top + jnp.log(tot)

    def prep_step(b, carry):
        prep_stage(jnp.asarray(b, jnp.int32))
        return carry

    def attn_step(b, carry):
        b = jnp.asarray(b, jnp.int32)
        p, mxs = softmax_stage(score_stage(b))
        output_stage(b, value_stage(b, p), mxs)
        return carry

    lax.fori_loop(0, nb, prep_step, 0, unroll=BLOCK_UNROLL)
    cache_stage()
    lax.fori_loop(0, nb, attn_step, 0, unroll=BLOCK_UNROLL)


def _attn_prompt_kernel(sink_ref, z_ref, gq_ref, gk_ref, cos_ref, sin_ref, *rest, seq, n_alias):
    (oa_ref, ob_ref, *cache_refs, qs_ref, kt_ref, vs_ref, kn_ref, obs_ref, lse_ref, bias_ref) = rest[n_alias:]
    m = pl.program_id(1)
    qi = lax.broadcasted_iota(jnp.int32, (BAND, 2 * BAND), 0)
    kj = lax.broadcasted_iota(jnp.int32, (BAND, 2 * BAND), 1)
    in_band = kj <= qi + BAND
    bias_ref[0] = jnp.where(in_band & (kj >= qi), 0.0, NEG)
    bias_ref[1] = jnp.where(in_band & (kj >= BAND), 0.0, NEG)

    for idx, (_, d) in enumerate(MIXERS):
        @pl.when(m == MIXER_ORDER.index(idx))
        def _(idx=idx, d=d):
            _attn_mixer(d, idx, seq, sink_ref, z_ref, gq_ref, gk_ref, cos_ref, sin_ref,
                        oa_ref, cache_refs[idx], qs_ref, kt_ref, vs_ref, kn_ref, obs_ref, lse_ref, bias_ref)

    @pl.when(m == N_MIX - 1)
    def _():
        ob_ref[...] = obs_ref[...]


def _attn_prompt(layer, depth, sink, z4, gq, gk, cos, sin, prev_caches):
    _, nb, seq, _ = z4.shape
    assert all(seq % (BAND * d) == 0 for _, d in MIXERS)
    per = ATT_W // LANES
    nq = QW // LANES
    tab = _const_spec((seq, LANES))
    mixer_at = lambda m: (N_MIX - m) % N_MIX
    assert all(MIXER_ORDER[m] == (N_MIX - m) % N_MIX for m in range(N_MIX))
    gain = pl.BlockSpec((None, 1, LANES), lambda b, m: (mixer_at(m), 0, 0))
    o_spec = pl.BlockSpec((None, nq, seq, LANES), lambda b, m: (b, 0, 0, 0))
    slabs = pltpu.VMEM((nq, seq, LANES), F32)
    wins = [min(win, seq) for win, _ in MIXERS]
    n_alias = 0 if prev_caches is None else len(prev_caches)
    n_in = 6
    outs = pl.pallas_call(
        functools.partial(_attn_prompt_kernel, seq=seq, n_alias=n_alias),
        grid=(nb, N_MIX),
        in_specs=[pl.BlockSpec(memory_space=pltpu.SMEM),
                  pl.BlockSpec((per, None, seq, LANES), lambda b, m: (mixer_at(m), b, 0, 0)),
                  gain, gain, tab, tab] + [pl.BlockSpec(memory_space=pl.ANY)] * n_alias,
        out_specs=[o_spec, o_spec]
                  + [pl.BlockSpec((None, None, 2, LANES, w), lambda b, m: (layer, b, 0, 0, 0)) for w in wins],
        out_shape=[jax.ShapeDtypeStruct((nb, nq, seq, LANES), F32),
                   jax.ShapeDtypeStruct((nb, nq, seq, LANES), F32)]
                  + [jax.ShapeDtypeStruct((depth, nb, 2, LANES, w), F32) for w in wins],
        scratch_shapes=[pltpu.VMEM((N_HEADS, seq, LANES), BF16),
                        pltpu.VMEM((seq // BAND, LANES, BAND), BF16),
                        pltpu.VMEM((seq, LANES), BF16),
                        pltpu.VMEM((seq, LANES), F32),
                        slabs, slabs,
                        pltpu.VMEM((2, BAND, 2 * BAND), F32)],
        input_output_aliases={n_in + j: 2 + j for j in range(n_alias)},
        compiler_params=_params(2),
        name="attn_prompt",
    )(sink, z4, gq, gk, cos, sin, *(prev_caches or ()))
    return outs[0], outs[1], tuple(outs[2:])


def _column_norm_rope(x, g, cos, sin):
    ss = jnp.sum(x * x, axis=0, keepdims=True)
    xn = x * lax.rsqrt(ss * (1.0 / HEAD_DIM) + EPS) * g
    x1, x2 = xn[0:ROT_HALF], xn[ROT_HALF:ROT_DIM]
    return jnp.concatenate([x1 * cos - x2 * sin, x2 * cos + x1 * sin, xn[ROT_DIM:]], axis=0)


def _decode_kernel(sink_ref, zt_ref, gq_ref, gk_ref, cos_ref, sin_ref,
                   ca_ref, c1_ref, c2_ref, c3_ref, *rest, bb, n_alias):
    rest = rest[n_alias:]
    oat_ref, obt_ref, na_ref, n1_ref, n2_ref, n3_ref, qn_ref, kn_ref, vn_ref = rest
    i = pl.program_id(0)
    nbatch = zt_ref.shape[1]
    cache_refs = (ca_ref, c1_ref, c2_ref, c3_ref)
    new_refs = (na_ref, n1_ref, n2_ref, n3_ref)

    @pl.when(i == 0)
    def _():
        cos, sin = cos_ref[...], sin_ref[...]
        for m in range(N_MIX):
            for h in range(N_HEADS):
                r = m * ATT_W + h * HEAD_DIM
                o = m * QW + h * HEAD_DIM
                qn_ref[o:o + HEAD_DIM, :] = _column_norm_rope(
                    zt_ref[r:r + HEAD_DIM, :], gq_ref[m * HEAD_DIM:(m + 1) * HEAD_DIM, :], cos, sin
                ) * (1.0 / math.sqrt(HEAD_DIM))
            for kvh in range(N_KV):
                r = m * ATT_W + QW + kvh * HEAD_DIM
                o = m * KVW + kvh * HEAD_DIM
                kn_ref[o:o + HEAD_DIM, :] = _column_norm_rope(
                    zt_ref[r:r + HEAD_DIM, :], gk_ref[m * HEAD_DIM:(m + 1) * HEAD_DIM, :], cos, sin)
            vn_ref[m * KVW:(m + 1) * KVW, :] = zt_ref[m * ATT_W + QW + KVW:(m + 1) * ATT_W, :]
        oat_ref[...] = jnp.zeros_like(oat_ref)
        obt_ref[...] = jnp.zeros_like(obt_ref)

    lane_b = lax.broadcasted_iota(jnp.int32, (1, nbatch), 1)

    def per_head(x):
        return jnp.concatenate([jnp.broadcast_to(x[h:h + 1], (HEAD_DIM, 1)) for h in range(N_HEADS)], axis=0)

    def slot_rows(col):
        return jnp.concatenate([col[h * HEAD_DIM:(h + 1) * HEAD_DIM] for h in SLOT_HEADS], axis=0)

    pick = (lax.broadcasted_iota(jnp.int32, (nbatch, bb * LANES), 0)
            == i * bb + lax.broadcasted_iota(jnp.int32, (nbatch, bb * LANES), 1) // LANES).astype(BF16)
    q_rep = _dot(qn_ref[...].astype(BF16), pick)

    def per_b(b):
        sel = lane_b == (i * bb + b)

        def column(ref):
            return jnp.sum(jnp.where(sel, ref[...], 0.0), axis=1, keepdims=True)

        kcol, vcol = column(kn_ref), column(vn_ref)
        run = None
        for m, (win, d) in enumerate(MIXERS):
            cref, nref = cache_refs[m], new_refs[m]
            last = lax.broadcasted_iota(jnp.int32, (HEAD_DIM, win), 1) == win - 1
            s_rows, new_rows, kv_new, vts = [], [], [], []
            for kvh in range(N_KV):
                kt, vt = cref[b, 0, kvh], cref[b, 1, kvh]
                r = m * KVW + kvh * HEAD_DIM
                kn, vn = kcol[r:r + HEAD_DIM], vcol[r:r + HEAD_DIM]
                nref[b, 0, kvh] = jnp.where(last, kn, pltpu.roll(kt, win - 1, 1))
                nref[b, 1, kvh] = jnp.where(last, vn, pltpu.roll(vt, win - 1, 1))
                for g in range(N_GROUP):
                    r = m * QW + SLOT_OF_HEAD[N_GROUP * kvh + g] * HEAD_DIM
                    qc = q_rep[r:r + HEAD_DIM, b * LANES:(b + 1) * LANES]
                    s_rows.append(jnp.sum(jnp.concatenate([qc] * (win // LANES), axis=1) * kt, axis=0, keepdims=True))
                    new_rows.append(jnp.sum(qc[:, 0:1] * kn, axis=0, keepdims=True))
                vts.append(vt)
                kv_new.append(vn)
            s, s_new = jnp.concatenate(s_rows, axis=0), jnp.concatenate(new_rows, axis=0)
            if d > 1:
                s = jnp.where((lax.broadcasted_iota(jnp.int32, (1, win), 1) % d) == 0, s, NEG)
            mx = jnp.maximum(jnp.max(s, axis=1, keepdims=True), s_new)
            if m == 0:
                sink = jnp.concatenate([jnp.full((1, 1), sink_ref[h], F32) for h in range(N_HEADS)], axis=0)
                mx = jnp.maximum(mx, sink)
            p, p_new = jnp.exp(s - mx), jnp.exp(s_new - mx)
            l = jnp.sum(p, axis=1, keepdims=True) + p_new
            if m == 0:
                l = l + jnp.exp(sink - mx)
            folded = []
            for h in range(N_HEADS):
                pv = p[h:h + 1] * vts[h // N_GROUP]
                f = pv[:, 0:LANES]
                for j in range(1, win // LANES):
                    f = f + pv[:, j * LANES:(j + 1) * LANES]
                folded.append(f)
            acc = jnp.sum(jnp.concatenate(folded, axis=0), axis=1, keepdims=True)
            acc = acc + per_head(p_new) * jnp.concatenate([kv_new[h // N_GROUP] for h in range(N_HEADS)], axis=0)
            if m == 0:
                oat_ref[...] = jnp.where(sel, slot_rows(acc / per_head(l)), oat_ref[...])
            elif run is None:
                run = (mx, l, acc)
            else:
                mx0, l0, acc0 = run
                top = jnp.maximum(mx0, mx)
                a0, a1 = jnp.exp(mx0 - top), jnp.exp(mx - top)
                run = (top, a0 * l0 + a1 * l, per_head(a0) * acc0 + per_head(a1) * acc)
        _, l, acc = run
        obt_ref[...] = jnp.where(sel, slot_rows(acc / per_head(l)), obt_ref[...])

    for b in range(bb):
        per_b(b)


def _decode(layer, sink, zt, gq_col, gk_col, cos_col, sin_col, caches_t, prev_out, *, bb):
    nbatch = zt.shape[1]
    assert nbatch % bb == 0
    cache_specs = [pl.BlockSpec((None, bb) + c.shape[2:], lambda i: (layer, i, 0, 0, 0, 0)) for c in caches_t]
    n_alias = 0 if prev_out is None else len(prev_out)
    alias_specs = [pl.BlockSpec(memory_space=pl.ANY)] * n_alias
    n_in = 6 + len(caches_t)
    outs = pl.pallas_call(
        functools.partial(_decode_kernel, bb=bb, n_alias=n_alias),
        grid=(nbatch // bb,),
        in_specs=[pl.BlockSpec(memory_space=pltpu.SMEM), _const_spec(zt.shape),
                  _const_spec(gq_col.shape), _const_spec(gk_col.shape),
                  _const_spec(cos_col.shape), _const_spec(sin_col.shape)] + cache_specs + alias_specs,
        out_specs=[_const_spec((QW, nbatch)), _const_spec((QW, nbatch))] + cache_specs,
        out_shape=[jax.ShapeDtypeStruct((QW, nbatch), F32), jax.ShapeDtypeStruct((QW, nbatch), F32)]
                  + [jax.ShapeDtypeStruct(c.shape, F32) for c in caches_t],
        scratch_shapes=[pltpu.VMEM((N_MIX * QW, nbatch), F32), pltpu.VMEM((N_MIX * KVW, nbatch), F32),
                        pltpu.VMEM((N_MIX * KVW, nbatch), F32)],
        input_output_aliases={n_in + j: 2 + j for j in range(n_alias)},
        compiler_params=_params(1),
        name="decode_attn_cache",
    )(sink, zt, gq_col, gk_col, cos_col, sin_col, *caches_t, *(prev_out or ()))
    return outs[0], outs[1], tuple(outs[2:])


def _merge_kernel(x_ref, h_ref, oa_ref, ob_ref, oc_ref, od_ref, wg_ref, bg_ref, wb_ref, wo_ref, out_ref):
    h = h_ref[...]
    slabs = lambda ref: jnp.concatenate([ref[c] for c in range(ref.shape[0])], axis=1)
    branches = (slabs(oa_ref), slabs(ob_ref), oc_ref[...], od_ref[...])
    merged = None
    for n, br in enumerate(branches):
        gate = jax.nn.sigmoid(_dot(h, wg_ref[n]) + bg_ref[n:n + 1])
        t = gate * _dot(br.astype(BF16), wb_ref[n])
        merged = t if merged is None else merged + t
    out_ref[...] = x_ref[...] + _dot(merged.astype(BF16), wo_ref[...])


def _merge(layer, x2d, h, oa, ob, oc, od, wg, bg, wb, wo, *, tm):
    m_rows = x2d.shape[0]
    seq = oa.shape[2]
    assert m_rows % tm == 0 and seq % tm == 0
    tps = seq // tm
    row = lambda w: pl.BlockSpec((tm, w), lambda i: (i, 0))
    slab = pl.BlockSpec((None, QW // LANES, tm, LANES), lambda i: (i // tps, 0, i % tps, 0))
    return pl.pallas_call(
        _merge_kernel,
        grid=(m_rows // tm,),
        in_specs=[row(D_MODEL), row(D_MODEL), slab, slab, row(BRANCH_W), row(BRANCH_W)]
                 + [_layer_spec(wg, layer), _const_spec(bg.shape), _layer_spec(wb, layer), _layer_spec(wo, layer)],
        out_specs=row(D_MODEL),
        out_shape=jax.ShapeDtypeStruct((m_rows, D_MODEL), F32),
        compiler_params=_params(1),
        name="branch_merge",
    )(x2d, h, oa, ob, oc, od, wg, bg, wb, wo)


def _ffn_kernel(x_ref, ln_ref, wi_ref, wo_ref, out_ref, act_ref):
    x = x_ref[...]
    h = _rms(x, ln_ref[...]).astype(BF16)
    for j in range(D_FF // FF_CHUNK):
        cols = slice(j * FF_CHUNK, (j + 1) * FF_CHUNK)
        gt = _dot(h, wi_ref[:, cols])
        up = _dot(h, wi_ref[:, D_FF + j * FF_CHUNK:D_FF + (j + 1) * FF_CHUNK])
        act_ref[:, cols] = (gt * jax.nn.sigmoid(gt) * up).astype(BF16)
    out_ref[...] = x + _dot(act_ref[...], wo_ref[...])


def _ffn(layer, x2d, ln, wi, wo, *, tm):
    m_rows = x2d.shape[0]
    assert m_rows % tm == 0 and D_FF % FF_CHUNK == 0
    row = pl.BlockSpec((tm, D_MODEL), lambda i: (i, 0))
    return pl.pallas_call(
        _ffn_kernel,
        grid=(m_rows // tm,),
        in_specs=[row, _const_spec(ln.shape), _layer_spec(wi, layer), _layer_spec(wo, layer)],
        out_specs=row,
        out_shape=jax.ShapeDtypeStruct((m_rows, D_MODEL), F32),
        scratch_shapes=[pltpu.VMEM((tm, D_FF), BF16)],
        compiler_params=_params(1),
        name="swiglu_ffn",
    )(x2d, ln, wi, wo)


def _rope_angles(pos):
    inv_freq = ROPE_THETA ** (-2.0 * jnp.arange(ROT_HALF, dtype=F32) / ROT_DIM)
    ang = pos[:, None] * inv_freq[None, :]
    return jnp.cos(ang), jnp.sin(ang)


def _rope_row_tables(pos):
    cos, sin = _rope_angles(pos)
    n = pos.shape[0]
    c = jnp.concatenate([cos, cos, jnp.ones((n, HEAD_DIM - ROT_DIM), F32)], axis=1)
    s = jnp.concatenate([-sin, sin, jnp.zeros((n, HEAD_DIM - ROT_DIM), F32)], axis=1)
    return jnp.tile(c, (1, N_KV)), jnp.tile(s, (1, N_KV))


def kernel(x_prompt, x_sample, cache_a, cache_b1, cache_b2, cache_b3, state_c, ln1, w_in, q_norm_a, k_norm_a,
           sink_a, q_norm_b, k_norm_b, conv_c, ws_d, bs_d, w_gate, b_gate, w_branch, w_o, ln2, w_ffn_in,
           w_ffn_out):
    bp, sp, _ = x_prompt.shape
    bs_n, ds, _ = x_sample.shape
    depth = ln1.shape[0]
    assert ds == 1 and sp >= CONV_W - 1
    caches = (cache_a, cache_b1, cache_b2, cache_b3)
    for c, (win, d) in zip(caches, MIXERS):
        assert c.shape[3] == win and win // d == BAND

    cos_p, sin_p = _rope_row_tables(jnp.arange(sp, dtype=F32))
    cos_s, sin_s = _rope_angles(PAST_LEN + jnp.arange(ds, dtype=F32))
    cos_col, sin_col = cos_s.reshape(ROT_HALF, 1), sin_s.reshape(ROT_HALF, 1)
    caches_t = tuple(jnp.transpose(c, (0, 1, 2, 4, 5, 3)) for c in caches)

    y_p = x_prompt.reshape(bp * sp, D_MODEL)
    y_s = x_sample.reshape(bs_n, D_MODEL)
    c_p, c_s, d_s = [], [], []
    kv_p = new_t = None
    assert SLOT_HEADS == tuple(kv * N_GROUP + g for g in range(N_GROUP) for kv in range(N_KV))
    w_att = w_in[:, :, :N_ATT].reshape(depth, D_MODEL, N_MIX, ATT_W)
    w_q = w_att[..., :QW].reshape(depth, D_MODEL, N_MIX, N_KV, N_GROUP, HEAD_DIM)
    w_q = jnp.swapaxes(w_q, 3, 4).reshape(depth, D_MODEL, N_MIX, QW)
    w_att_b = jnp.concatenate([w_q, w_att[..., QW:]], axis=-1).reshape(depth, D_MODEL, N_ATT).astype(BF16)
    w_rest_b = w_in[:, :, N_ATT:].astype(BF16)
    w_ab = w_branch[:, :2].reshape(depth, 2, N_KV, N_GROUP, HEAD_DIM, D_MODEL)
    w_ab = jnp.swapaxes(w_ab, 2, 3).reshape(depth, 2, QW, D_MODEL)
    wb_b = jnp.concatenate([w_ab, w_branch[:, 2:]], axis=1).astype(BF16)
    wg_b, wo_b = w_gate.astype(BF16), w_o.astype(BF16)
    wfi_b, wfo_b = w_ffn_in.astype(BF16), w_ffn_out.astype(BF16)
    for l in range(depth):
        ln1_l, ln2_l = ln1[l][None], ln2[l][None]
        gq_all = jnp.concatenate([q_norm_a[l][None], q_norm_b[l]], axis=0)
        gk_all = jnp.concatenate([k_norm_a[l][None], k_norm_b[l]], axis=0)
        bst = jnp.repeat(bs_d[l].T, HEAD_DIM, axis=1)

        h, zatt, oc, od, newc = _inproj_prompt(l, y_p, ln1_l, w_att_b, w_rest_b, conv_c[l], ws_d[l], bst,
                                               seq=sp, tm=ROW_TILE)
        oa, ob, kv_p = _attn_prompt(l, depth, sink_a[l], zatt.reshape(N_ATT // LANES, bp, sp, LANES),
                                    jnp.tile(gq_all, (1, N_KV))[:, None], jnp.tile(gk_all, (1, N_KV))[:, None],
                                    cos_p, sin_p, kv_p)
        x1 = _merge(l, y_p, h, oa, ob, oc, od, wg_b, b_gate[l], wb_b, wo_b, tm=ROW_TILE)
        y_p = _ffn(l, x1, ln2_l, wfi_b, wfo_b, tm=ROW_TILE)
        c_p.append(newc)

        s0, s1 = state_c[l][:, 0], state_c[l][:, 1]
        w00 = jnp.repeat(ws_d[l][:, 0, 0], HEAD_DIM)[None]
        b0 = jnp.repeat(bs_d[l][:, 0], HEAD_DIM)[None]
        h, zatt, oc, od, zc, vd = _inproj_sample(l, y_s, ln1_l, w_att_b, w_rest_b, conv_c[l], s0, s1, w00, b0)
        oat, obt, new_t = _decode(l, sink_a[l], zatt.T, gq_all.reshape(-1, 1), gk_all.reshape(-1, 1),
                                  cos_col, sin_col, caches_t, new_t, bb=DEC_BLOCK)
        to_slabs = lambda t: t.reshape(QW // LANES, LANES, bs_n).transpose(0, 2, 1)[None]
        x1 = _merge(l, y_s, h, to_slabs(oat), to_slabs(obt), oc, od, wg_b, b_gate[l], wb_b, wo_b, tm=bs_n)
        y_s = _ffn(l, x1, ln2_l, wfi_b, wfo_b, tm=bs_n)
        c_s.append(jnp.stack([s1, zc], axis=1))
        d_s.append(vd[:, None])

    def prompt_cache(m):
        t = kv_p[m]
        return jnp.transpose(t.reshape(depth, bp, 2, N_KV, HEAD_DIM, t.shape[-1]), (0, 1, 2, 5, 3, 4))

    sample_cache = [jnp.transpose(t, (0, 1, 2, 5, 3, 4)) for t in new_t]
    return (y_p.reshape(bp, sp, D_MODEL), y_s.reshape(bs_n, ds, D_MODEL),
            prompt_cache(0), sample_cache[0], prompt_cache(1), sample_cache[1],
            prompt_cache(2), sample_cache[2], prompt_cache(3), sample_cache[3],
            jnp.stack(c_p), jnp.stack(c_s), jnp.stack(d_s))
```

```python
import functools
import math

import jax
import jax.numpy as jnp
from jax import lax
from jax.experimental import pallas as pl
from jax.experimental.pallas import tpu as pltpu

F32 = jnp.float32
BF16 = jnp.bfloat16

D_MODEL = 1024
HEAD_DIM = 64
ROT_DIM = HEAD_DIM // 4
ROT_HALF = ROT_DIM // 2
ROPE_THETA = 500000.0
BRANCH_W = 384
N_HEADS = BRANCH_W // HEAD_DIM
N_KV = 2
N_GROUP = N_HEADS // N_KV
QW = N_HEADS * HEAD_DIM
KVW = N_KV * HEAD_DIM
ATT_W = QW + 2 * KVW
WIN_A = 128
DIL_GROUPS = ((128, 1), (512, 4), (2048, 16))
MIXERS = ((WIN_A, 1),) + DIL_GROUPS
N_MIX = len(MIXERS)
MIXER_ORDER = (0, 3, 2, 1)
BAND = 128
LANES = 128
SLOT_HEADS = (0, 3, 1, 4, 2, 5)
SLOT_OF_HEAD = tuple(SLOT_HEADS.index(h) for h in range(6))
CONV_W = 3
CHUNK = 128
N_SG = 6
N_BRANCH = 4
N_ATT = ATT_W * N_MIX
N_IN = N_ATT + 5 * BRANCH_W
D_FF = 2816
FF_CHUNK = 256
PAST_LEN = 16384
EPS = 1e-6
NEG = -1e30

VMEM_LIMIT_V7X = 56 * 1024 * 1024
ROW_TILE = 512
DEC_BLOCK = 2
BLOCK_UNROLL = 16
PROJ_CHUNK = 1280


def _params(n_axes, vmem=VMEM_LIMIT_V7X):
    return pltpu.CompilerParams(dimension_semantics=("arbitrary",) * n_axes, vmem_limit_bytes=vmem)


def _const_spec(shape):
    nd = len(shape)
    return pl.BlockSpec(shape, lambda *_: (0,) * nd)


def _layer_spec(stacked, layer):
    shape = stacked.shape[1:]
    return pl.BlockSpec((None,) + shape, lambda *_: (layer,) + (0,) * len(shape))


def _rms(x, g):
    return x * lax.rsqrt(jnp.mean(x * x, axis=-1, keepdims=True) + EPS) * g


def _dot(a, b):
    return jnp.dot(a, b, preferred_element_type=F32)


def _project_slabs(h, w_ref):
    n = w_ref.shape[1]
    pieces = []
    for lo in range(0, n, PROJ_CHUNK):
        hi = min(lo + PROJ_CHUNK, n)
        z = _dot(h, w_ref[:, lo:hi])
        pieces += [z[:, c * LANES:(c + 1) * LANES] for c in range((hi - lo) // LANES)]
    return pieces


def _project_branches(x_ref, ln_ref, wr_ref, h_ref):
    h = _rms(x_ref[...], ln_ref[...]).astype(BF16)
    h_ref[...] = h
    pieces = _project_slabs(h, wr_ref)
    per = BRANCH_W // LANES
    return h, [jnp.concatenate(pieces[j * per:(j + 1) * per], axis=1) for j in range(5)]


def _project_attention(h, wa_ref, zatt_ref, slabs):
    pieces = _project_slabs(h, wa_ref)
    if slabs:
        for c, piece in enumerate(pieces):
            zatt_ref[c] = piece
    else:
        zatt_ref[...] = jnp.concatenate(pieces, axis=1)


def _inproj_prompt_kernel(x_ref, ln_ref, wa_ref, wr_ref, conv_ref, ws_ref, bst_ref,
                          h_ref, zatt_ref, oc_ref, od_ref, newc_ref, carry_ref, *, tm, tiles_per_seq):
    i = pl.program_id(0)
    h, (gate_b, gate_c, x_c, u_d, v_d) = _project_branches(x_ref, ln_ref, wr_ref, h_ref)

    @pl.when(i % tiles_per_seq == 0)
    def _():
        carry_ref[...] = jnp.zeros_like(carry_ref)

    zc = gate_c * x_c
    carry = carry_ref[...]
    row = lax.broadcasted_iota(jnp.int32, zc.shape, 0)
    z1 = jnp.where(row == 0, carry[1:2], pltpu.roll(zc, 1, 0))
    z2 = jnp.where(row == 0, carry[0:1], jnp.where(row == 1, carry[1:2], pltpu.roll(zc, 2, 0)))
    conv = conv_ref[0:1] * z2 + conv_ref[1:2] * z1 + conv_ref[2:3] * zc
    oc_ref[...] = gate_b * conv
    tail = zc[tm - 2:tm]
    carry_ref[...] = tail
    newc_ref[...] = tail

    tri = (lax.broadcasted_iota(jnp.int32, (CHUNK, CHUNK), 0)
           >= lax.broadcasted_iota(jnp.int32, (CHUNK, CHUNK), 1))
    vb = v_d.astype(BF16)
    nchunk = tm // CHUNK
    mixed = []
    for g in range(N_SG):
        wm = jnp.where(tri, ws_ref[g], 0.0).astype(BF16)
        vg = jnp.concatenate([vb[c * CHUNK:(c + 1) * CHUNK, g * HEAD_DIM:(g + 1) * HEAD_DIM]
                              for c in range(nchunk)], axis=1)
        mixed.append(_dot(wm, vg))
    for c in range(nchunk):
        rows = slice(c * CHUNK, (c + 1) * CHUNK)
        sv = jnp.concatenate([mixed[g][:, c * HEAD_DIM:(c + 1) * HEAD_DIM] for g in range(N_SG)], axis=1)
        od_ref[rows, :] = u_d[rows] * (sv + bst_ref[...])

    _project_attention(h, wa_ref, zatt_ref, True)


def _inproj_sample_kernel(x_ref, ln_ref, wa_ref, wr_ref, conv_ref, s0_ref, s1_ref, w00_ref, b0_ref,
                          h_ref, zatt_ref, oc_ref, od_ref, zc_ref, vd_ref):
    h, (gate_b, gate_c, x_c, u_d, v_d) = _project_branches(x_ref, ln_ref, wr_ref, h_ref)
    _project_attention(h, wa_ref, zatt_ref, False)
    zc = gate_c * x_c
    conv = conv_ref[0:1] * s0_ref[...] + conv_ref[1:2] * s1_ref[...] + conv_ref[2:3] * zc
    oc_ref[...] = gate_b * conv
    zc_ref[...] = zc
    od_ref[...] = u_d * (v_d * w00_ref[...] + b0_ref[...])
    vd_ref[...] = v_d


def _inproj_prompt(layer, x2d, ln, w_att, w_rest, conv, ws, bst, *, seq, tm):
    m_rows = x2d.shape[0]
    assert m_rows % seq == 0 and seq % tm == 0 and tm % CHUNK == 0
    tps = seq // tm
    row = lambda w: pl.BlockSpec((tm, w), lambda i: (i, 0))
    return pl.pallas_call(
        functools.partial(_inproj_prompt_kernel, tm=tm, tiles_per_seq=tps),
        grid=(m_rows // tm,),
        in_specs=[row(D_MODEL), _const_spec((1, D_MODEL)), _layer_spec(w_att, layer), _layer_spec(w_rest, layer),
                  _const_spec((CONV_W, BRANCH_W)), _const_spec((N_SG, CHUNK, CHUNK)),
                  _const_spec((CHUNK, BRANCH_W))],
        out_specs=[row(D_MODEL), pl.BlockSpec((N_ATT // LANES, tm, LANES), lambda i: (0, i, 0)),
                   row(BRANCH_W), row(BRANCH_W),
                   pl.BlockSpec((None, CONV_W - 1, BRANCH_W), lambda i: (i // tps, 0, 0))],
        out_shape=[jax.ShapeDtypeStruct((m_rows, D_MODEL), BF16),
                   jax.ShapeDtypeStruct((N_ATT // LANES, m_rows, LANES), F32),
                   jax.ShapeDtypeStruct((m_rows, BRANCH_W), F32),
                   jax.ShapeDtypeStruct((m_rows, BRANCH_W), F32),
                   jax.ShapeDtypeStruct((m_rows // seq, CONV_W - 1, BRANCH_W), F32)],
        scratch_shapes=[pltpu.VMEM((CONV_W - 1, BRANCH_W), F32)],
        compiler_params=_params(1),
        name="inproj_prompt",
    )(x2d, ln, w_att, w_rest, conv, ws, bst)


def _inproj_sample(layer, x2d, ln, w_att, w_rest, conv, s0, s1, w00, b0):
    n = x2d.shape[0]
    full = lambda w: _const_spec((n, w))
    return pl.pallas_call(
        _inproj_sample_kernel,
        grid=(1,),
        in_specs=[full(D_MODEL), _const_spec((1, D_MODEL)), _layer_spec(w_att, layer), _layer_spec(w_rest, layer),
                  _const_spec((CONV_W, BRANCH_W)), full(BRANCH_W), full(BRANCH_W),
                  _const_spec((1, BRANCH_W)), _const_spec((1, BRANCH_W))],
        out_specs=[full(D_MODEL), full(N_ATT)] + [full(BRANCH_W)] * 4,
        out_shape=[jax.ShapeDtypeStruct((n, D_MODEL), BF16), jax.ShapeDtypeStruct((n, N_ATT), F32)]
                  + [jax.ShapeDtypeStruct((n, BRANCH_W), F32)] * 4,
        compiler_params=_params(1),
        name="inproj_sample",
    )(x2d, ln, w_att, w_rest, conv, s0, s1, w00, b0)


def _segment_ones(n):
    r = lax.broadcasted_iota(jnp.int32, (n, n), 0) // HEAD_DIM
    c = lax.broadcasted_iota(jnp.int32, (n, n), 1) // HEAD_DIM
    return (r == c).astype(BF16)


def _rotary_partner_matrix():
    j = lax.broadcasted_iota(jnp.int32, (LANES, LANES), 0)
    l = lax.broadcasted_iota(jnp.int32, (LANES, LANES), 1)
    e = l % HEAD_DIM
    first, second = e < ROT_HALF, (e >= ROT_HALF) & (e < ROT_DIM)
    return ((first & (j == l + ROT_HALF)) | (second & (j == l - ROT_HALF))).astype(BF16)


def _head_norm_rope(xs, gains, ones, partner, cos, sin):
    rows = xs[0].shape[0]
    ss = _dot(jnp.concatenate([(x * x).astype(BF16) for x in xs], axis=0), ones)
    xn = [x * lax.rsqrt(ss[c * rows:(c + 1) * rows] * (1.0 / HEAD_DIM) + EPS) * g
          for c, (x, g) in enumerate(zip(xs, gains))]
    xr = _dot(jnp.concatenate([t.astype(BF16) for t in xn], axis=0), partner)
    return [t * cos + xr[c * rows:(c + 1) * rows] * sin for c, t in enumerate(xn)]


def _attn_mixer(d, idx, seq, sink_ref, z_ref, gq_ref, gk_ref, cos_ref, sin_ref,
                oa_ref, cache_ref, qs_ref, kt_ref, vs_ref, kn_ref, obs_ref, lse_ref, bias_ref):
    with_sink, first_group = idx == 0, idx == MIXER_ORDER[1]
    nblk = seq // (BAND * d)
    has_prev = nblk > 1
    shift = nblk.bit_length() - 1
    ones = _segment_ones(LANES)
    partner = _rotary_partner_matrix()
    low = lax.broadcasted_iota(jnp.int32, (BAND, LANES), 1) < HEAD_DIM

    def natural_rows(it):
        if d == 1:
            return pl.ds(pl.multiple_of(it * BAND, BAND), BAND)
        return pl.ds((it >> shift) + (it & (nblk - 1)) * (BAND * d), BAND, stride=d)

    nb = seq // BAND
    nk = 2 * BAND if has_prev else BAND
    nq = QW // LANES
    ones_cols = jnp.ones((nk, LANES), BF16)

    def block_pos(b):
        return pl.ds(pl.multiple_of(b * BAND, BAND), BAND)

    def prev_block(b):
        return jnp.maximum(b - 1, 0)

    def prep_stage(it):
        rows = natural_rows(it)
        pos = block_pos(it)
        *qn, kn = _head_norm_rope([z_ref[c, rows, :] for c in range(nq + 1)],
                                  [gq_ref[...]] * nq + [gk_ref[...]], ones, partner,
                                  cos_ref[rows, :], sin_ref[rows, :])
        for c in range(nq):
            qc = (qn[c] * (1.0 / math.sqrt(HEAD_DIM))).astype(BF16)
            qs_ref[2 * c, pos, :] = jnp.where(low, qc, jnp.zeros_like(qc))
            qs_ref[2 * c + 1, pos, :] = jnp.where(low, jnp.zeros_like(qc), qc)
        kn_ref[rows, :] = kn
        kt_ref[it] = kn.T.astype(BF16)
        vs_ref[pos, :] = z_ref[nq + 1, rows, :].astype(BF16)

    def cache_stage():
        win = cache_ref.shape[2]
        for j in range(win // BAND):
            t0 = seq - win + j * BAND
            cache_ref[0, :, j * BAND:(j + 1) * BAND] = kn_ref[t0:t0 + BAND, :].T
            cache_ref[1, :, j * BAND:(j + 1) * BAND] = z_ref[nq + 1, t0:t0 + BAND, :].T

    def score_stage(b):
        q6 = jnp.concatenate([qs_ref[j, block_pos(b), :] for j in range(N_HEADS)], axis=0)
        if has_prev:
            first = (b & (nblk - 1)) == 0
            kt = jnp.concatenate([kt_ref[prev_block(b)], kt_ref[b]], axis=1)
            bias = bias_ref[jnp.where(first, 1, 0)]
        else:
            kt = kt_ref[b]
            bias = bias_ref[1, :, BAND:]
        return _dot(q6, kt) + jnp.concatenate([bias] * N_HEADS, axis=0)

    def softmax_stage(s):
        mx = jnp.max(s, axis=1, keepdims=True)
        p = jnp.exp(s - mx).astype(BF16)
        return p, [jnp.where(low, mx[2 * c * BAND:(2 * c + 1) * BAND], mx[(2 * c + 1) * BAND:(2 * c + 2) * BAND])
                   for c in range(nq)]

    def value_stage(b, p):
        v = vs_ref[block_pos(b), :]
        if has_prev:
            v = jnp.concatenate([vs_ref[block_pos(prev_block(b)), :], v], axis=0)
        v_aug = jnp.concatenate([v, ones_cols], axis=1)
        return _dot(p, v_aug)

    def output_stage(b, r2, mxs):
        pos = block_pos(b)
        rows = natural_rows(b)
        for c in range(nq):
            ra, rb = slice(2 * c * BAND, (2 * c + 1) * BAND), slice((2 * c + 1) * BAND, (2 * c + 2) * BAND)
            dn = jnp.where(low, r2[ra, LANES:], r2[rb, LANES:])
            mx = mxs[c]
            if with_sink:
                dn = dn + jnp.exp(jnp.where(low, sink_ref[SLOT_HEADS[2 * c]], sink_ref[SLOT_HEADS[2 * c + 1]]) - mx)
            o = jnp.where(low, r2[ra, :LANES], r2[rb, :LANES]) / dn
            if with_sink:
                oa_ref[c, pos, :] = o
                continue
            lse = mx + jnp.log(dn)
            if first_group:
                obs_ref[c, rows, :] = o
                lse_ref[c, rows, :] = lse
            else:
                lse_old = lse_ref[c, rows, :]
                top = jnp.maximum(lse_old, lse)
                w_old = jnp.exp(lse_old - top)
                w_new = jnp.exp(lse - top)
                tot = w_old + w_new
                obs_ref[c, rows, :] = (w_old * obs_ref[c, rows, :] + w_new * o) / tot
                lse_ref[c, rows, :] = top + jnp.log(tot)

    def prep_step(b, carry):
        prep_stage(jnp.asarray(b, jnp.int32))
        return carry

    def attn_step(b, carry):
        b = jnp.asarray(b, jnp.int32)
        p, mxs = softmax_stage(score_stage(b))
        output_stage(b, value_stage(b, p), mxs)
        return carry

    lax.fori_loop(0, nb, prep_step, 0, unroll=BLOCK_UNROLL)
    cache_stage()
    lax.fori_loop(0, nb, attn_step, 0, unroll=BLOCK_UNROLL)


def _attn_prompt_kernel(sink_ref, z_ref, gq_ref, gk_ref, cos_ref, sin_ref, *rest, seq, n_alias):
    (oa_ref, ob_ref, *cache_refs, qs_ref, kt_ref, vs_ref, kn_ref, obs_ref, lse_ref, bias_ref) = rest[n_alias:]
    m = pl.program_id(1)
    qi = lax.broadcasted_iota(jnp.int32, (BAND, 2 * BAND), 0)
    kj = lax.broadcasted_iota(jnp.int32, (BAND, 2 * BAND), 1)
    in_band = kj <= qi + BAND
    bias_ref[0] = jnp.where(in_band & (kj >= qi), 0.0, NEG)
    bias_ref[1] = jnp.where(in_band & (kj >= BAND), 0.0, NEG)

    for idx, (_, d) in enumerate(MIXERS):
        @pl.when(m == MIXER_ORDER.index(idx))
        def _(idx=idx, d=d):
            _attn_mixer(d, idx, seq, sink_ref, z_ref, gq_ref, gk_ref, cos_ref, sin_ref,
                        oa_ref, cache_refs[idx], qs_ref, kt_ref, vs_ref, kn_ref, obs_ref, lse_ref, bias_ref)

    @pl.when(m == N_MIX - 1)
    def _():
        ob_ref[...] = obs_ref[...]


def _attn_prompt(layer, depth, sink, z4, gq, gk, cos, sin, prev_caches):
    _, nb, seq, _ = z4.shape
    assert all(seq % (BAND * d) == 0 for _, d in MIXERS)
    per = ATT_W // LANES
    nq = QW // LANES
    tab = _const_spec((seq, LANES))
    mixer_at = lambda m: (N_MIX - m) % N_MIX
    assert all(MIXER_ORDER[m] == (N_MIX - m) % N_MIX for m in range(N_MIX))
    gain = pl.BlockSpec((None, 1, LANES), lambda b, m: (mixer_at(m), 0, 0))
    o_spec = pl.BlockSpec((None, nq, seq, LANES), lambda b, m: (b, 0, 0, 0))
    slabs = pltpu.VMEM((nq, seq, LANES), F32)
    wins = [min(win, seq) for win, _ in MIXERS]
    n_alias = 0 if prev_caches is None else len(prev_caches)
    n_in = 6
    outs = pl.pallas_call(
        functools.partial(_attn_prompt_kernel, seq=seq, n_alias=n_alias),
        grid=(nb, N_MIX),
        in_specs=[pl.BlockSpec(memory_space=pltpu.SMEM),
                  pl.BlockSpec((per, None, seq, LANES), lambda b, m: (mixer_at(m), b, 0, 0)),
                  gain, gain, tab, tab] + [pl.BlockSpec(memory_space=pl.ANY)] * n_alias,
        out_specs=[o_spec, o_spec]
                  + [pl.BlockSpec((None, None, 2, LANES, w), lambda b, m: (layer, b, 0, 0, 0)) for w in wins],
        out_shape=[jax.ShapeDtypeStruct((nb, nq, seq, LANES), F32),
                   jax.ShapeDtypeStruct((nb, nq, seq, LANES), F32)]
                  + [jax.ShapeDtypeStruct((depth, nb, 2, LANES, w), F32) for w in wins],
        scratch_shapes=[pltpu.VMEM((N_HEADS, seq, LANES), BF16),
                        pltpu.VMEM((seq // BAND, LANES, BAND), BF16),
                        pltpu.VMEM((seq, LANES), BF16),
                        pltpu.VMEM((seq, LANES), F32),
                        slabs, slabs,
                        pltpu.VMEM((2, BAND, 2 * BAND), F32)],
        input_output_aliases={n_in + j: 2 + j for j in range(n_alias)},
        compiler_params=_params(2),
        name="attn_prompt",
    )(sink, z4, gq, gk, cos, sin, *(prev_caches or ()))
    return outs[0], outs[1], tuple(outs[2:])


def _column_norm_rope(x, g, cos, sin):
    ss = jnp.sum(x * x, axis=0, keepdims=True)
    xn = x * lax.rsqrt(ss * (1.0 / HEAD_DIM) + EPS) * g
    x1, x2 = xn[0:ROT_HALF], xn[ROT_HALF:ROT_DIM]
    return jnp.concatenate([x1 * cos - x2 * sin, x2 * cos + x1 * sin, xn[ROT_DIM:]], axis=0)


def _decode_kernel(sink_ref, zt_ref, gq_ref, gk_ref, cos_ref, sin_ref,
                   ca_ref, c1_ref, c2_ref, c3_ref, *rest, bb, n_alias):
    rest = rest[n_alias:]
    oat_ref, obt_ref, na_ref, n1_ref, n2_ref, n3_ref, qn_ref, kn_ref, vn_ref = rest
    i = pl.program_id(0)
    nbatch = zt_ref.shape[1]
    cache_refs = (ca_ref, c1_ref, c2_ref, c3_ref)
    new_refs = (na_ref, n1_ref, n2_ref, n3_ref)

    @pl.when(i == 0)
    def _():
        cos, sin = cos_ref[...], sin_ref[...]
        for m in range(N_MIX):
            for h in range(N_HEADS):
                r = m * ATT_W + h * HEAD_DIM
                o = m * QW + h * HEAD_DIM
                qn_ref[o:o + HEAD_DIM, :] = _column_norm_rope(
                    zt_ref[r:r + HEAD_DIM, :], gq_ref[m * HEAD_DIM:(m + 1) * HEAD_DIM, :], cos, sin
                ) * (1.0 / math.sqrt(HEAD_DIM))
            for kvh in range(N_KV):
                r = m * ATT_W + QW + kvh * HEAD_DIM
                o = m * KVW + kvh * HEAD_DIM
                kn_ref[o:o + HEAD_DIM, :] = _column_norm_rope(
                    zt_ref[r:r + HEAD_DIM, :], gk_ref[m * HEAD_DIM:(m + 1) * HEAD_DIM, :], cos, sin)
            vn_ref[m * KVW:(m + 1) * KVW, :] = zt_ref[m * ATT_W + QW + KVW:(m + 1) * ATT_W, :]
        oat_ref[...] = jnp.zeros_like(oat_ref)
        obt_ref[...] = jnp.zeros_like(obt_ref)

    lane_b = lax.broadcasted_iota(jnp.int32, (1, nbatch), 1)

    def per_head(x):
        return jnp.concatenate([jnp.broadcast_to(x[h:h + 1], (HEAD_DIM, 1)) for h in range(N_HEADS)], axis=0)

    def slot_rows(col):
        return jnp.concatenate([col[h * HEAD_DIM:(h + 1) * HEAD_DIM] for h in SLOT_HEADS], axis=0)

    pick = (lax.broadcasted_iota(jnp.int32, (nbatch, bb * LANES), 0)
            == i * bb + lax.broadcasted_iota(jnp.int32, (nbatch, bb * LANES), 1) // LANES).astype(BF16)
    q_rep = _dot(qn_ref[...].astype(BF16), pick)

    def per_b(b):
        sel = lane_b == (i * bb + b)

        def column(ref):
            return jnp.sum(jnp.where(sel, ref[...], 0.0), axis=1, keepdims=True)

        kcol, vcol = column(kn_ref), column(vn_ref)
        run = None
        for m, (win, d) in enumerate(MIXERS):
            cref, nref = cache_refs[m], new_refs[m]
            last = lax.broadcasted_iota(jnp.int32, (HEAD_DIM, win), 1) == win - 1
            s_rows, new_rows, kv_new, vts = [], [], [], []
            for kvh in range(N_KV):
                kt, vt = cref[b, 0, kvh], cref[b, 1, kvh]
                r = m * KVW + kvh * HEAD_DIM
                kn, vn = kcol[r:r + HEAD_DIM], vcol[r:r + HEAD_DIM]
                nref[b, 0, kvh] = jnp.where(last, kn, pltpu.roll(kt, win - 1, 1))
                nref[b, 1, kvh] = jnp.where(last, vn, pltpu.roll(vt, win - 1, 1))
                for g in range(N_GROUP):
                    r = m * QW + SLOT_OF_HEAD[N_GROUP * kvh + g] * HEAD_DIM
                    qc = q_rep[r:r + HEAD_DIM, b * LANES:(b + 1) * LANES]
                    s_rows.append(jnp.sum(jnp.concatenate([qc] * (win // LANES), axis=1) * kt, axis=0, keepdims=True))
                    new_rows.append(jnp.sum(qc[:, 0:1] * kn, axis=0, keepdims=True))
                vts.append(vt)
                kv_new.append(vn)
            s, s_new = jnp.concatenate(s_rows, axis=0), jnp.concatenate(new_rows, axis=0)
            if d > 1:
                s = jnp.where((lax.broadcasted_iota(jnp.int32, (1, win), 1) % d) == 0, s, NEG)
            mx = jnp.maximum(jnp.max(s, axis=1, keepdims=True), s_new)
            if m == 0:
                sink = jnp.concatenate([jnp.full((1, 1), sink_ref[h], F32) for h in range(N_HEADS)], axis=0)
                mx = jnp.maximum(mx, sink)
            p, p_new = jnp.exp(s - mx), jnp.exp(s_new - mx)
            l = jnp.sum(p, axis=1, keepdims=True) + p_new
            if m == 0:
                l = l + jnp.exp(sink - mx)
            folded = []
            for h in range(N_HEADS):
                pv = p[h:h + 1] * vts[h // N_GROUP]
                f = pv[:, 0:LANES]
                for j in range(1, win // LANES):
                    f = f + pv[:, j * LANES:(j + 1) * LANES]
                folded.append(f)
            acc = jnp.sum(jnp.concatenate(folded, axis=0), axis=1, keepdims=True)
            acc = acc + per_head(p_new) * jnp.concatenate([kv_new[h // N_GROUP] for h in range(N_HEADS)], axis=0)
            if m == 0:
                oat_ref[...] = jnp.where(sel, slot_rows(acc / per_head(l)), oat_ref[...])
            elif run is None:
                run = (mx, l, acc)
            else:
                mx0, l0, acc0 = run
                top = jnp.maximum(mx0, mx)
                a0, a1 = jnp.exp(mx0 - top), jnp.exp(mx - top)
                run = (top, a0 * l0 + a1 * l, per_head(a0) * acc0 + per_head(a1) * acc)
        _, l, acc = run
        obt_ref[...] = jnp.where(sel, slot_rows(acc / per_head(l)), obt_ref[...])

    for b in range(bb):
        per_b(b)


def _decode(layer, sink, zt, gq_col, gk_col, cos_col, sin_col, caches_t, prev_out, *, bb):
    nbatch = zt.shape[1]
    assert nbatch % bb == 0
    cache_specs = [pl.BlockSpec((None, bb) + c.shape[2:], lambda i: (layer, i, 0, 0, 0, 0)) for c in caches_t]
    n_alias = 0 if prev_out is None else len(prev_out)
    alias_specs = [pl.BlockSpec(memory_space=pl.ANY)] * n_alias
    n_in = 6 + len(caches_t)
    outs = pl.pallas_call(
        functools.partial(_decode_kernel, bb=bb, n_alias=n_alias),
        grid=(nbatch // bb,),
        in_specs=[pl.BlockSpec(memory_space=pltpu.SMEM), _const_spec(zt.shape),
                  _const_spec(gq_col.shape), _const_spec(gk_col.shape),
                  _const_spec(cos_col.shape), _const_spec(sin_col.shape)] + cache_specs + alias_specs,
        out_specs=[_const_spec((QW, nbatch)), _const_spec((QW, nbatch))] + cache_specs,
        out_shape=[jax.ShapeDtypeStruct((QW, nbatch), F32), jax.ShapeDtypeStruct((QW, nbatch), F32)]
                  + [jax.ShapeDtypeStruct(c.shape, F32) for c in caches_t],
        scratch_shapes=[pltpu.VMEM((N_MIX * QW, nbatch), F32), pltpu.VMEM((N_MIX * KVW, nbatch), F32),
                        pltpu.VMEM((N_MIX * KVW, nbatch), F32)],
        input_output_aliases={n_in + j: 2 + j for j in range(n_alias)},
        compiler_params=_params(1),
        name="decode_attn_cache",
    )(sink, zt, gq_col, gk_col, cos_col, sin_col, *caches_t, *(prev_out or ()))
    return outs[0], outs[1], tuple(outs[2:])


def _merge_kernel(x_ref, h_ref, oa_ref, ob_ref, oc_ref, od_ref, wg_ref, bg_ref, wb_ref, wo_ref, out_ref):
    h = h_ref[...]
    slabs = lambda ref: jnp.concatenate([ref[c] for c in range(ref.shape[0])], axis=1)
    branches = (slabs(oa_ref), slabs(ob_ref), oc_ref[...], od_ref[...])
    merged = None
    for n, br in enumerate(branches):
        gate = jax.nn.sigmoid(_dot(h, wg_ref[n]) + bg_ref[n:n + 1])
        t = gate * _dot(br.astype(BF16), wb_ref[n])
        merged = t if merged is None else merged + t
    out_ref[...] = x_ref[...] + _dot(merged.astype(BF16), wo_ref[...])


def _merge(layer, x2d, h, oa, ob, oc, od, wg, bg, wb, wo, *, tm):
    m_rows = x2d.shape[0]
    seq = oa.shape[2]
    assert m_rows % tm == 0 and seq % tm == 0
    tps = seq // tm
    row = lambda w: pl.BlockSpec((tm, w), lambda i: (i, 0))
    slab = pl.BlockSpec((None, QW // LANES, tm, LANES), lambda i: (i // tps, 0, i % tps, 0))
    return pl.pallas_call(
        _merge_kernel,
        grid=(m_rows // tm,),
        in_specs=[row(D_MODEL), row(D_MODEL), slab, slab, row(BRANCH_W), row(BRANCH_W)]
                 + [_layer_spec(wg, layer), _const_spec(bg.shape), _layer_spec(wb, layer), _layer_spec(wo, layer)],
        out_specs=row(D_MODEL),
        out_shape=jax.ShapeDtypeStruct((m_rows, D_MODEL), F32),
        compiler_params=_params(1),
        name="branch_merge",
    )(x2d, h, oa, ob, oc, od, wg, bg, wb, wo)


def _ffn_kernel(x_ref, ln_ref, wi_ref, wo_ref, out_ref, act_ref):
    x = x_ref[...]
    h = _rms(x, ln_ref[...]).astype(BF16)
    for j in range(D_FF // FF_CHUNK):
        cols = slice(j * FF_CHUNK, (j + 1) * FF_CHUNK)
        gt = _dot(h, wi_ref[:, cols])
        up = _dot(h, wi_ref[:, D_FF + j * FF_CHUNK:D_FF + (j + 1) * FF_CHUNK])
        act_ref[:, cols] = (gt * jax.nn.sigmoid(gt) * up).astype(BF16)
    out_ref[...] = x + _dot(act_ref[...], wo_ref[...])


def _ffn(layer, x2d, ln, wi, wo, *, tm):
    m_rows = x2d.shape[0]
    assert m_rows % tm == 0 and D_FF % FF_CHUNK == 0
    row = pl.BlockSpec((tm, D_MODEL), lambda i: (i, 0))
    return pl.pallas_call(
        _ffn_kernel,
        grid=(m_rows // tm,),
        in_specs=[row, _const_spec(ln.shape), _layer_spec(wi, layer), _layer_spec(wo, layer)],
        out_specs=row,
        out_shape=jax.ShapeDtypeStruct((m_rows, D_MODEL), F32),
        scratch_shapes=[pltpu.VMEM((tm, D_FF), BF16)],
        compiler_params=_params(1),
        name="swiglu_ffn",
    )(x2d, ln, wi, wo)


def _rope_angles(pos):
    inv_freq = ROPE_THETA ** (-2.0 * jnp.arange(ROT_HALF, dtype=F32) / ROT_DIM)
    ang = pos[:, None] * inv_freq[None, :]
    return jnp.cos(ang), jnp.sin(ang)


def _rope_row_tables(pos):
    cos, sin = _rope_angles(pos)
    n = pos.shape[0]
    c = jnp.concatenate([cos, cos, jnp.ones((n, HEAD_DIM - ROT_DIM), F32)], axis=1)
    s = jnp.concatenate([-sin, sin, jnp.zeros((n, HEAD_DIM - ROT_DIM), F32)], axis=1)
    return jnp.tile(c, (1, N_KV)), jnp.tile(s, (1, N_KV))


def kernel(x_prompt, x_sample, cache_a, cache_b1, cache_b2, cache_b3, state_c, ln1, w_in, q_norm_a, k_norm_a,
           sink_a, q_norm_b, k_norm_b, conv_c, ws_d, bs_d, w_gate, b_gate, w_branch, w_o, ln2, w_ffn_in,
           w_ffn_out):
    bp, sp, _ = x_prompt.shape
    bs_n, ds, _ = x_sample.shape
    depth = ln1.shape[0]
    assert ds == 1 and sp >= CONV_W - 1
    caches = (cache_a, cache_b1, cache_b2, cache_b3)
    for c, (win, d) in zip(caches, MIXERS):
        assert c.shape[3] == win and win // d == BAND

    cos_p, sin_p = _rope_row_tables(jnp.arange(sp, dtype=F32))
    cos_s, sin_s = _rope_angles(PAST_LEN + jnp.arange(ds, dtype=F32))
    cos_col, sin_col = cos_s.reshape(ROT_HALF, 1), sin_s.reshape(ROT_HALF, 1)
    caches_t = tuple(jnp.transpose(c, (0, 1, 2, 4, 5, 3)) for c in caches)

    y_p = x_prompt.reshape(bp * sp, D_MODEL)
    y_s = x_sample.reshape(bs_n, D_MODEL)
    c_p, c_s, d_s = [], [], []
    kv_p = new_t = None
    assert SLOT_HEADS == tuple(kv * N_GROUP + g for g in range(N_GROUP) for kv in range(N_KV))
    w_att = w_in[:, :, :N_ATT].reshape(depth, D_MODEL, N_MIX, ATT_W)
    w_q = w_att[..., :QW].reshape(depth, D_MODEL, N_MIX, N_KV, N_GROUP, HEAD_DIM)
    w_q = jnp.swapaxes(w_q, 3, 4).reshape(depth, D_MODEL, N_MIX, QW)
    w_att_b = jnp.concatenate([w_q, w_att[..., QW:]], axis=-1).reshape(depth, D_MODEL, N_ATT).astype(BF16)
    w_rest_b = w_in[:, :, N_ATT:].astype(BF16)
    w_ab = w_branch[:, :2].reshape(depth, 2, N_KV, N_GROUP, HEAD_DIM, D_MODEL)
    w_ab = jnp.swapaxes(w_ab, 2, 3).reshape(depth, 2, QW, D_MODEL)
    wb_b = jnp.concatenate([w_ab, w_branch[:, 2:]], axis=1).astype(BF16)
    wg_b, wo_b = w_gate.astype(BF16), w_o.astype(BF16)
    wfi_b, wfo_b = w_ffn_in.astype(BF16), w_ffn_out.astype(BF16)
    for l in range(depth):
        ln1_l, ln2_l = ln1[l][None], ln2[l][None]
        gq_all = jnp.concatenate([q_norm_a[l][None], q_norm_b[l]], axis=0)
        gk_all = jnp.concatenate([k_norm_a[l][None], k_norm_b[l]], axis=0)
        bst = jnp.repeat(bs_d[l].T, HEAD_DIM, axis=1)

        h, zatt, oc, od, newc = _inproj_prompt(l, y_p, ln1_l, w_att_b, w_rest_b, conv_c[l], ws_d[l], bst,
                                               seq=sp, tm=ROW_TILE)
        oa, ob, kv_p = _attn_prompt(l, depth, sink_a[l], zatt.reshape(N_ATT // LANES, bp, sp, LANES),
                                    jnp.tile(gq_all, (1, N_KV))[:, None], jnp.tile(gk_all, (1, N_KV))[:, None],
                                    cos_p, sin_p, kv_p)
        x1 = _merge(l, y_p, h, oa, ob, oc, od, wg_b, b_gate[l], wb_b, wo_b, tm=ROW_TILE)
        y_p = _ffn(l, x1, ln2_l, wfi_b, wfo_b, tm=ROW_TILE)
        c_p.append(newc)

        s0, s1 = state_c[l][:, 0], state_c[l][:, 1]
        w00 = jnp.repeat(ws_d[l][:, 0, 0], HEAD_DIM)[None]
        b0 = jnp.repeat(bs_d[l][:, 0], HEAD_DIM)[None]
        h, zatt, oc, od, zc, vd = _inproj_sample(l, y_s, ln1_l, w_att_b, w_rest_b, conv_c[l], s0, s1, w00, b0)
        oat, obt, new_t = _decode(l, sink_a[l], zatt.T, gq_all.reshape(-1, 1), gk_all.reshape(-1, 1),
                                  cos_col, sin_col, caches_t, new_t, bb=DEC_BLOCK)
        to_slabs = lambda t: t.reshape(QW // LANES, LANES, bs_n).transpose(0, 2, 1)[None]
        x1 = _merge(l, y_s, h, to_slabs(oat), to_slabs(obt), oc, od, wg_b, b_gate[l], wb_b, wo_b, tm=bs_n)
        y_s = _ffn(l, x1, ln2_l, wfi_b, wfo_b, tm=bs_n)
        c_s.append(jnp.stack([s1, zc], axis=1))
        d_s.append(vd[:, None])

    def prompt_cache(m):
        t = kv_p[m]
        return jnp.transpose(t.reshape(depth, bp, 2, N_KV, HEAD_DIM, t.shape[-1]), (0, 1, 2, 5, 3, 4))

    sample_cache = [jnp.transpose(t, (0, 1, 2, 5, 3, 4)) for t in new_t]
    return (y_p.reshape(bp, sp, D_MODEL), y_s.reshape(bs_n, ds, D_MODEL),
            prompt_cache(0), sample_cache[0], prompt_cache(1), sample_cache[1],
            prompt_cache(2), sample_cache[2], prompt_cache(3), sample_cache[3],
            jnp.stack(c_p), jnp.stack(c_s), jnp.stack(d_s))
```
